```python
import math
import jax, jax.numpy as jnp
from jax import lax
import numpy as np

D_MODEL = 2048
BATCH = 2
SEQ = 4096
DEPTH = 1
DEC_BATCH = 128
DEC_SEQ = 1
PAST_LEN = 2048
PAGE_SIZE = 128

N_HEADS = 8
HEAD_DIM = 64
V_DIM = 2 * HEAD_DIM
ROT_DIM = HEAD_DIM // 4
ROPE_THETA = 500000.0
Q_BLOCK = 128
D_QK = N_HEADS * 2 * HEAD_DIM
D_ATTN_V = N_HEADS * V_DIM
SSM_GROUP = 16
D_SSM = D_MODEL // 2
N_GROUPS = D_SSM // SSM_GROUP
SSM_STATE = 64
D_FF = 5632
SPLITS = (D_QK, 2 * D_QK, 2 * D_QK + D_ATTN_V, 2 * D_QK + D_ATTN_V + D_SSM,
          2 * D_QK + D_ATTN_V + D_SSM + D_MODEL)
D_IN = 2 * D_QK + D_ATTN_V + D_SSM + 2 * D_MODEL
EPS = 1e-6

kernel_name = 'macaron_s5_diffattn_gated_hybrid_step'

F32 = jnp.float32


def rms_norm(x, g):
    xf = x.astype(F32)
    y = xf * lax.rsqrt(jnp.mean(xf * xf, axis=-1, keepdims=True) + EPS)
    return (y * g.astype(F32)).astype(x.dtype)


def swiglu(x, w1, w3, w2):
    return (jax.nn.silu(x @ w1) * (x @ w3)) @ w2


def rotary(x, pos):
    half = ROT_DIM // 2
    inv = ROPE_THETA ** (-jnp.arange(half, dtype=F32) / half)
    ang = pos.astype(F32)[:, None] * inv[None, :]
    cos = jnp.cos(ang)[:, None, None, :]
    sin = jnp.sin(ang)[:, None, None, :]
    xr = x[..., :ROT_DIM].astype(F32)
    x1, x2 = xr[..., :half], xr[..., half:]
    rot = jnp.concatenate([x1 * cos - x2 * sin, x2 * cos + x1 * sin], axis=-1)
    return jnp.concatenate([rot.astype(x.dtype), x[..., ROT_DIM:]], axis=-1)


def diff_attend(q, k, v, q_pos, k_pos, lam):
    s = jnp.einsum('bqhcd,bkhcd->bhcqk', q, k, preferred_element_type=F32) * (HEAD_DIM ** -0.5)
    mask = k_pos[None, :] <= q_pos[:, None]
    s = jnp.where(mask, s, jnp.finfo(F32).min)
    p = jax.nn.softmax(s, axis=-1)
    w = p[:, :, 0] - lam * p[:, :, 1]
    return jnp.einsum('bhqk,bkhv->bqhv', w.astype(v.dtype), v)


def s5_scan(u, h_re0, h_im0, a_re, a_im, log_dt, b_re, b_im, c_re, c_im, d_skip):
    dt = jnp.exp(log_dt.astype(F32))[:, None]
    ar, ai = a_re.astype(F32), a_im.astype(F32)
    mag = jnp.exp(dt * ar)
    lb_re, lb_im = mag * jnp.cos(dt * ai), mag * jnp.sin(dt * ai)
    num_re, num_im = lb_re - 1.0, lb_im
    den = ar * ar + ai * ai
    coef_re = (num_re * ar + num_im * ai) / den
    coef_im = (num_im * ar - num_re * ai) / den
    br, bi = b_re.astype(F32), b_im.astype(F32)
    bb_re = coef_re[..., None] * br - coef_im[..., None] * bi
    bb_im = coef_re[..., None] * bi + coef_im[..., None] * br
    uf = u.astype(F32)
    bu_re = jnp.einsum('gnc,btgc->btgn', bb_re, uf)
    bu_im = jnp.einsum('gnc,btgc->btgn', bb_im, uf)
    hr0, hi0 = h_re0.astype(F32), h_im0.astype(F32)
    bu_re = bu_re.at[:, 0].add(lb_re * hr0 - lb_im * hi0)
    bu_im = bu_im.at[:, 0].add(lb_re * hi0 + lb_im * hr0)
    A_re = jnp.broadcast_to(lb_re, bu_re.shape)
    A_im = jnp.broadcast_to(lb_im, bu_im.shape)

    def combine(e1, e2):
        a1r, a1i, b1r, b1i = e1
        a2r, a2i, b2r, b2i = e2
        return (a2r * a1r - a2i * a1i, a2r * a1i + a2i * a1r,
                a2r * b1r - a2i * b1i + b2r, a2r * b1i + a2i * b1r + b2i)

    _, _, h_re, h_im = lax.associative_scan(combine, (A_re, A_im, bu_re, bu_im), axis=1)
    y = (jnp.einsum('gcn,btgn->btgc', c_re.astype(F32), h_re)
         - jnp.einsum('gcn,btgn->btgc', c_im.astype(F32), h_im)
         + d_skip.astype(F32) * uf)
    return y.astype(u.dtype), h_re[:, -1].astype(h_re0.dtype), h_im[:, -1].astype(h_im0.dtype)


def layer(x, pos, past_k, past_v, h_re0, h_im0, p, lam_init):
    B, T, _ = x.shape
    x = x + 0.5 * swiglu(rms_norm(x, p['norm_ffn1']), p['ffn1_w1'], p['ffn1_w3'], p['ffn1_w2'])
    xn = rms_norm(x, p['norm_mix'])
    proj = xn @ p['w_in']
    q, k, v, u, ga, gb = jnp.split(proj, SPLITS, axis=-1)
    q = rotary(q.reshape(B, T, N_HEADS, 2, HEAD_DIM), pos)
    k = rotary(k.reshape(B, T, N_HEADS, 2, HEAD_DIM), pos)
    v = v.reshape(B, T, N_HEADS, V_DIM)
    lam = (jnp.exp(jnp.sum(p['lam_q1'].astype(F32) * p['lam_k1'].astype(F32)))
           - jnp.exp(jnp.sum(p['lam_q2'].astype(F32) * p['lam_k2'].astype(F32))) + lam_init)
    if past_k is None:
        nb = T // Q_BLOCK
        qb = q.reshape(B, nb, Q_BLOCK, N_HEADS, 2, HEAD_DIM).swapaxes(0, 1)
        starts = jnp.arange(nb, dtype=jnp.int32) * Q_BLOCK

        def one_block(args):
            qi, s0 = args
            return diff_attend(qi, k, v, s0 + jnp.arange(Q_BLOCK, dtype=jnp.int32), pos, lam)

        o = lax.map(one_block, (qb, starts))
        o = o.swapaxes(0, 1).reshape(B, T, N_HEADS, V_DIM)
    else:
        k_all = jnp.concatenate([past_k.reshape(B, -1, N_HEADS, 2, HEAD_DIM), k], axis=1)
        v_all = jnp.concatenate([past_v, v], axis=1)
        k_pos = jnp.arange(k_all.shape[1], dtype=jnp.int32)
        o = diff_attend(q, k_all, v_all, pos, k_pos, lam)
    o = rms_norm(o, p['g_subln']) * (1.0 - lam_init)
    attn_branch = o.reshape(B, T, D_ATTN_V) @ p['w_attn_out']
    ys, h_re, h_im = s5_scan(u.reshape(B, T, N_GROUPS, SSM_GROUP), h_re0, h_im0,
                             p['ssm_a_re'], p['ssm_a_im'], p['ssm_log_dt'], p['ssm_b_re'],
                             p['ssm_b_im'], p['ssm_c_re'], p['ssm_c_im'], p['ssm_d'])
    ya = jax.nn.gelu(ys.reshape(B, T, D_SSM), approximate=False)
    ssm_branch = (ya * jax.nn.sigmoid(ya @ p['w_glu'])) @ p['w_ssm_out']
    mix = jax.nn.sigmoid(ga) * ssm_branch + jax.nn.sigmoid(gb) * attn_branch
    x = x + mix @ p['w_o']
    x = x + 0.5 * swiglu(rms_norm(x, p['norm_ffn2']), p['ffn2_w1'], p['ffn2_w3'], p['ffn2_w2'])
    return x, k.reshape(B, T, N_HEADS, 2 * HEAD_DIM), v, h_re, h_im


def setup_inputs(seed: int = 0) -> dict:
    key = jax.random.key(seed)
    ks = iter(jax.random.split(key, 48))

    def nrm(shape, scale):
        return jax.random.normal(next(ks), shape, F32) * scale

    def gain(shape):
        return 1.0 + nrm(shape, 0.02)

    n_pages = PAST_LEN // PAGE_SIZE
    n_used = DEC_BATCH * n_pages
    n_pool = n_used + n_used // 4
    x_prompt = nrm((BATCH, SEQ, D_MODEL), 1.0)
    x_sample = nrm((DEC_BATCH, DEC_SEQ, D_MODEL), 1.0)
    cache_k = nrm((DEPTH, n_pool, PAGE_SIZE, N_HEADS, 2 * HEAD_DIM), 1.0)
    cache_v = nrm((DEPTH, n_pool, PAGE_SIZE, N_HEADS, V_DIM), 1.0)
    state_ssm_re = nrm((DEPTH, DEC_BATCH, N_GROUPS, SSM_STATE), 1.0)
    state_ssm_im = nrm((DEPTH, DEC_BATCH, N_GROUPS, SSM_STATE), 1.0)
    page_table = jax.random.permutation(next(ks), n_pool)[:n_used].reshape(DEC_BATCH, n_pages).astype(jnp.int32)
    L = DEPTH
    a_im0 = math.pi * jnp.arange(SSM_STATE, dtype=F32)
    return {
        'x_prompt': x_prompt, 'x_sample': x_sample,
        'cache_k': cache_k, 'cache_v': cache_v,
        'state_ssm_re': state_ssm_re, 'state_ssm_im': state_ssm_im,
        'page_table': page_table,
        'norm_ffn1': gain((L, D_MODEL)),
        'ffn1_w1': nrm((L, D_MODEL, D_FF), D_MODEL ** -0.5),
        'ffn1_w3': nrm((L, D_MODEL, D_FF), D_MODEL ** -0.5),
        'ffn1_w2': nrm((L, D_FF, D_MODEL), D_FF ** -0.5),
        'norm_mix': gain((L, D_MODEL)),
        'w_in': nrm((L, D_MODEL, D_IN), D_MODEL ** -0.5),
        'lam_q1': nrm((L, HEAD_DIM), 0.1), 'lam_k1': nrm((L, HEAD_DIM), 0.1),
        'lam_q2': nrm((L, HEAD_DIM), 0.1), 'lam_k2': nrm((L, HEAD_DIM), 0.1),
        'g_subln': gain((L, V_DIM)),
        'w_attn_out': nrm((L, D_ATTN_V, D_MODEL), D_ATTN_V ** -0.5),
        'ssm_a_re': -0.5 + nrm((L, N_GROUPS, SSM_STATE), 0.01),
        'ssm_a_im': a_im0 + nrm((L, N_GROUPS, SSM_STATE), 0.01),
        'ssm_log_dt': jax.random.uniform(next(ks), (L, N_GROUPS), F32, math.log(1e-3), math.log(1e-1)),
        'ssm_b_re': nrm((L, N_GROUPS, SSM_STATE, SSM_GROUP), (2 * SSM_GROUP) ** -0.5),
        'ssm_b_im': nrm((L, N_GROUPS, SSM_STATE, SSM_GROUP), (2 * SSM_GROUP) ** -0.5),
        'ssm_c_re': nrm((L, N_GROUPS, SSM_GROUP, SSM_STATE), (2 * SSM_STATE) ** -0.5),
        'ssm_c_im': nrm((L, N_GROUPS, SSM_GROUP, SSM_STATE), (2 * SSM_STATE) ** -0.5),
        'ssm_d': nrm((L, N_GROUPS, SSM_GROUP), 1.0),
        'w_glu': nrm((L, D_SSM, D_SSM), D_SSM ** -0.5),
        'w_ssm_out': nrm((L, D_SSM, D_MODEL), D_SSM ** -0.5),
        'w_o': nrm((L, D_MODEL, D_MODEL), D_MODEL ** -0.5),
        'norm_ffn2': gain((L, D_MODEL)),
        'ffn2_w1': nrm((L, D_MODEL, D_FF), D_MODEL ** -0.5),
        'ffn2_w3': nrm((L, D_MODEL, D_FF), D_MODEL ** -0.5),
        'ffn2_w2': nrm((L, D_FF, D_MODEL), D_FF ** -0.5),
        'norm_final': gain((D_MODEL,)),
    }


def reference(x_prompt, x_sample, cache_k, cache_v, state_ssm_re, state_ssm_im, page_table,
              norm_ffn1, ffn1_w1, ffn1_w3, ffn1_w2, norm_mix, w_in, lam_q1, lam_k1, lam_q2, lam_k2,
              g_subln, w_attn_out, ssm_a_re, ssm_a_im, ssm_log_dt, ssm_b_re, ssm_b_im, ssm_c_re,
              ssm_c_im, ssm_d, w_glu, w_ssm_out, w_o, norm_ffn2, ffn2_w1, ffn2_w3, ffn2_w2, norm_final):
    n_pages = PAST_LEN // PAGE_SIZE
    pos_p = jnp.arange(SEQ, dtype=jnp.int32)
    pos_s = PAST_LEN + jnp.arange(DEC_SEQ, dtype=jnp.int32)
    h_p = x_prompt
    h_s = x_sample
    kp_l, vp_l, hrp_l, hip_l, ks_l, vs_l, hrs_l, his_l = [], [], [], [], [], [], [], []
    for l in range(DEPTH):
        p = dict(norm_ffn1=norm_ffn1[l], ffn1_w1=ffn1_w1[l], ffn1_w3=ffn1_w3[l], ffn1_w2=ffn1_w2[l],
                 norm_mix=norm_mix[l], w_in=w_in[l], lam_q1=lam_q1[l], lam_k1=lam_k1[l],
                 lam_q2=lam_q2[l], lam_k2=lam_k2[l], g_subln=g_subln[l], w_attn_out=w_attn_out[l],
                 ssm_a_re=ssm_a_re[l], ssm_a_im=ssm_a_im[l], ssm_log_dt=ssm_log_dt[l],
                 ssm_b_re=ssm_b_re[l], ssm_b_im=ssm_b_im[l], ssm_c_re=ssm_c_re[l], ssm_c_im=ssm_c_im[l],
                 ssm_d=ssm_d[l], w_glu=w_glu[l], w_ssm_out=w_ssm_out[l], w_o=w_o[l],
                 norm_ffn2=norm_ffn2[l], ffn2_w1=ffn2_w1[l], ffn2_w3=ffn2_w3[l], ffn2_w2=ffn2_w2[l])
        lam_init = 0.8 - 0.6 * math.exp(-0.3 * l)
        z = jnp.zeros((BATCH, N_GROUPS, SSM_STATE), state_ssm_re.dtype)
        h_p, kp, vp, hrp, hip = layer(h_p, pos_p, None, None, z, z, p, lam_init)
        past_k = cache_k[l][page_table].reshape(DEC_BATCH, n_pages * PAGE_SIZE, N_HEADS, 2 * HEAD_DIM)
        past_v = cache_v[l][page_table].reshape(DEC_BATCH, n_pages * PAGE_SIZE, N_HEADS, V_DIM)
        h_s, ks_, vs_, hrs, his = layer(h_s, pos_s, past_k, past_v, state_ssm_re[l], state_ssm_im[l], p, lam_init)
        kp_l.append(kp); vp_l.append(vp); hrp_l.append(hrp); hip_l.append(hip)
        ks_l.append(ks_); vs_l.append(vs_); hrs_l.append(hrs); his_l.append(his)
    y_prompt = rms_norm(h_p, norm_final)
    y_sample = rms_norm(h_s, norm_final)
    return (y_prompt, y_sample, jnp.stack(kp_l), jnp.stack(vp_l), jnp.stack(hrp_l), jnp.stack(hip_l),
            jnp.stack(ks_l), jnp.stack(vs_l), jnp.stack(hrs_l), jnp.stack(his_l))
```

```python
import functools
import math

import jax
import jax.numpy as jnp
from jax import lax
from jax.experimental import pallas as pl
from jax.experimental.pallas import tpu as pltpu

F32 = jnp.float32
BF16 = jnp.bfloat16

HEAD_DIM = 64
ROT_DIM = HEAD_DIM // 4
ROPE_THETA = 500000.0
EPS = 1e-6
SSM_GROUP = 16
SSM_GROUP_BLOCK = 8

V7X_LANES = 128
V7X_SUBLANES = 8
V7X_VMEM_BYTES = 64 * 1024 * 1024
VMEM_LIMIT = V7X_VMEM_BYTES * 3 // 4

NEG_BIG = -1e30


def _dot(a, b):
    return jnp.dot(a, b, preferred_element_type=F32)


def _dot_nt(a, b):
    return lax.dot_general(a, b, (((1,), (1,)), ((), ())), preferred_element_type=F32)


def _rms(x, g):
    return x * lax.rsqrt(jnp.mean(x * x, axis=-1, keepdims=True) + EPS) * g


def _params(sem):
    return pltpu.CompilerParams(dimension_semantics=sem, vmem_limit_bytes=VMEM_LIMIT)


def _ffn_kernel(x_ref, g_ref, w1_ref, w3_ref, w2_ref, gf_ref, o_ref, xn_s, acc_s, *, final_norm):
    f = pl.program_id(1)

    @pl.when(f == 0)
    def _():
        xn_s[...] = _rms(x_ref[...], g_ref[...]).astype(BF16)
        acc_s[...] = jnp.zeros_like(acc_s)

    xn = xn_s[...]
    h1 = _dot(xn, w1_ref[...])
    h3 = _dot(xn, w3_ref[...])
    hh = (h1 * jax.nn.sigmoid(h1) * h3).astype(BF16)
    acc_s[...] += _dot(hh, w2_ref[...])

    @pl.when(f == pl.num_programs(1) - 1)
    def _():
        y = x_ref[...] + 0.5 * acc_s[...]
        if final_norm:
            y = _rms(y, gf_ref[...])
        o_ref[...] = y


def _ffn(x, g, w1, w3, w2, gf, *, final_norm, tm, tf):
    m, d = x.shape
    dff = w1.shape[1]
    return pl.pallas_call(
        functools.partial(_ffn_kernel, final_norm=final_norm),
        grid=(m // tm, dff // tf),
        in_specs=[
            pl.BlockSpec((tm, d), lambda i, f: (i, 0)),
            pl.BlockSpec((1, d), lambda i, f: (0, 0)),
            pl.BlockSpec((d, tf), lambda i, f: (0, f)),
            pl.BlockSpec((d, tf), lambda i, f: (0, f)),
            pl.BlockSpec((tf, d), lambda i, f: (f, 0)),
            pl.BlockSpec((1, d), lambda i, f: (0, 0)),
        ],
        out_specs=pl.BlockSpec((tm, d), lambda i, f: (i, 0)),
        out_shape=jax.ShapeDtypeStruct((m, d), F32),
        scratch_shapes=[pltpu.VMEM((tm, d), BF16), pltpu.VMEM((tm, d), F32)],
        compiler_params=_params(("parallel", "arbitrary")),
        name="ffn",
    )(x, g, w1, w3, w2, gf)


def _rotate(a, cos_ref, s1_ref, s2_ref):
    cos, s1, s2 = cos_ref[...], s1_ref[...], s2_ref[...]
    half = ROT_DIM // 2
    out = []
    for h in range(a.shape[1] // V7X_LANES):
        blk = a[:, h * V7X_LANES:(h + 1) * V7X_LANES]
        up = pltpu.roll(blk, V7X_LANES - half, 1)
        dn = pltpu.roll(blk, half, 1)
        out.append(blk * cos + up * s1 + dn * s2)
    return jnp.concatenate(out, axis=1)


def _proj_qk_kernel(x_ref, g_ref, w_ref, cos_ref, s1_ref, s2_ref, q_ref, k_ref, kb_ref, xn_s):
    j = pl.program_id(1)

    @pl.when(j == 0)
    def _():
        xn_s[...] = _rms(x_ref[...], g_ref[...]).astype(BF16)

    r = _rotate(_dot(xn_s[...], w_ref[...]), cos_ref, s1_ref, s2_ref)

    @pl.when(j == 0)
    def _():
        q_ref[...] = (r * (HEAD_DIM ** -0.5)).astype(BF16)

    @pl.when(j == 1)
    def _():
        k_ref[...] = r
        kb_ref[...] = r.astype(BF16)


def _proj_v_kernel(x_ref, g_ref, w_ref, v_ref, vb_ref):
    a = _dot(_rms(x_ref[...], g_ref[...]).astype(BF16), w_ref[...])
    v_ref[...] = a
    vb_ref[...] = a.astype(BF16)


def _proj_u_kernel(x_ref, g_ref, w_ref, u_ref):
    u_ref[...] = _dot(_rms(x_ref[...], g_ref[...]).astype(BF16), w_ref[...])


def _proj_gate_kernel(x_ref, g_ref, w_ref, s_ref, xn_s):
    @pl.when(pl.program_id(1) == 0)
    def _():
        xn_s[...] = _rms(x_ref[...], g_ref[...]).astype(BF16)

    s_ref[...] = jax.nn.sigmoid(_dot(xn_s[...], w_ref[...])).astype(BF16)


def _project(x, g, w_in, cos_t, s1_t, s2_t, *, tm, d_qk, d_v, d_ssm):
    m, d = x.shape
    tn = d_qk
    assert d_v == tn and d_ssm == tn and (w_in.shape[1] - 4 * tn) % tn == 0
    n_gate = (w_in.shape[1] - 4 * tn) // tn
    t_rows = cos_t.shape[0]
    n_t = t_rows // tm
    x_spec = pl.BlockSpec((tm, d), lambda i, j: (i, 0))
    g_spec = pl.BlockSpec((1, d), lambda i, j: (0, 0))
    tab_spec = pl.BlockSpec((tm, V7X_LANES), lambda i, j: (i % n_t, 0))
    row_spec = pl.BlockSpec((tm, tn), lambda i, j: (i, 0))

    q, k, kb = pl.pallas_call(
        _proj_qk_kernel,
        grid=(m // tm, 2),
        in_specs=[x_spec, g_spec, pl.BlockSpec((d, tn), lambda i, j: (0, j)), tab_spec, tab_spec, tab_spec],
        out_specs=[row_spec, row_spec, row_spec],
        out_shape=[jax.ShapeDtypeStruct((m, tn), BF16), jax.ShapeDtypeStruct((m, tn), F32),
                   jax.ShapeDtypeStruct((m, tn), BF16)],
        scratch_shapes=[pltpu.VMEM((tm, d), BF16)],
        compiler_params=_params(("parallel", "arbitrary")),
        name="proj_qk",
    )(x, g, w_in, cos_t, s1_t, s2_t)

    v, vb = pl.pallas_call(
        _proj_v_kernel,
        grid=(m // tm, 1),
        in_specs=[x_spec, g_spec, pl.BlockSpec((d, tn), lambda i, j: (0, 2))],
        out_specs=[row_spec, row_spec],
        out_shape=[jax.ShapeDtypeStruct((m, tn), F32), jax.ShapeDtypeStruct((m, tn), BF16)],
        compiler_params=_params(("parallel", "arbitrary")),
        name="proj_v",
    )(x, g, w_in)

    u = pl.pallas_call(
        _proj_u_kernel,
        grid=(m // tm, 1),
        in_specs=[x_spec, g_spec, pl.BlockSpec((d, tn), lambda i, j: (0, 3))],
        out_specs=row_spec,
        out_shape=jax.ShapeDtypeStruct((m, tn), F32),
        compiler_params=_params(("parallel", "arbitrary")),
        name="proj_u",
    )(x, g, w_in)

    gates = pl.pallas_call(
        _proj_gate_kernel,
        grid=(m // tm, n_gate),
        in_specs=[x_spec, g_spec, pl.BlockSpec((d, tn), lambda i, j: (0, 4 + j))],
        out_specs=pl.BlockSpec((tm, tn), lambda i, j: (i, j)),
        out_shape=jax.ShapeDtypeStruct((m, n_gate * tn), BF16),
        scratch_shapes=[pltpu.VMEM((tm, d), BF16)],
        compiler_params=_params(("parallel", "arbitrary")),
        name="proj_gate",
    )(x, g, w_in)
    return q, k, kb, v, vb, u, gates


def _lam(lam_ref, lam_init):
    a = jnp.sum(lam_ref[0:1, :] * lam_ref[1:2, :], axis=-1, keepdims=True)
    b = jnp.sum(lam_ref[2:3, :] * lam_ref[3:4, :], axis=-1, keepdims=True)
    return jnp.exp(a) - jnp.exp(b) + lam_init


def _subln(o, gs, lam_init):
    return _rms(o, gs) * (1.0 - lam_init)


def _split_components(q):
    lane = lax.broadcasted_iota(jnp.int32, q.shape, 1)
    zero = jnp.zeros_like(q)
    return jnp.concatenate([jnp.where(lane < HEAD_DIM, q, zero), jnp.where(lane >= HEAD_DIM, q, zero)], axis=0)


def _softmax_update(s, v, m_s, l_s, acc_s):
    m_prev = m_s[...]
    m_new = jnp.maximum(m_prev, jnp.max(s, axis=-1, keepdims=True))
    alpha = jnp.exp(m_prev - m_new)
    p = jnp.exp(s - jnp.tile(m_new, (1, s.shape[1] // V7X_LANES)))
    l_s[...] = alpha * l_s[...] + jnp.sum(p, axis=-1, keepdims=True)
    acc_s[...] = alpha * acc_s[...] + _dot(p.astype(BF16), v)
    m_s[...] = m_new


def _attn_prompt_kernel(lam_ref, gs_ref, q_ref, k_ref, v_ref, o_ref, qq_s, m_s, l_s, acc_s, *, tq, lam_init):
    i = pl.program_id(2)
    qq_s[...] = _split_components(q_ref[...])
    m_s[...] = jnp.full_like(m_s, NEG_BIG)
    l_s[...] = jnp.zeros_like(l_s)
    acc_s[...] = jnp.zeros_like(acc_s)

    def step(j, masked):
        start = pl.multiple_of(j * tq, tq)
        kt = k_ref[pl.ds(start, tq), :]
        vt = v_ref[pl.ds(start, tq), :]
        s = _dot_nt(qq_s[...], kt)
        if masked:
            row = lax.broadcasted_iota(jnp.int32, s.shape, 0)
            row = jnp.where(row >= tq, row - tq, row)
            col = lax.broadcasted_iota(jnp.int32, s.shape, 1)
            s = jnp.where(col <= row, s, NEG_BIG)
        _softmax_update(s, vt, m_s, l_s, acc_s)

    def body(j, c):
        step(j, False)
        return c

    lax.fori_loop(0, i, body, 0)
    step(i, True)

    o1 = acc_s[0:tq, :] / l_s[0:tq, :]
    o2 = acc_s[tq:2 * tq, :] / l_s[tq:2 * tq, :]
    o = o1 - _lam(lam_ref, lam_init) * o2
    o_ref[...] = _subln(o, gs_ref[...], lam_init).astype(BF16)


def _attn_prompt(qb, kb, vb, lam4, gs, *, batch, seq, tq, lam_init):
    m, dq = qb.shape
    n_heads = dq // V7X_LANES
    nq = seq // tq
    return pl.pallas_call(
        functools.partial(_attn_prompt_kernel, tq=tq, lam_init=lam_init),
        grid=(batch, n_heads, nq),
        in_specs=[
            pl.BlockSpec(lam4.shape, lambda b, h, i: (0, 0)),
            pl.BlockSpec((1, V7X_LANES), lambda b, h, i: (0, 0)),
            pl.BlockSpec((tq, V7X_LANES), lambda b, h, i: (b * nq + i, h)),
            pl.BlockSpec((seq, V7X_LANES), lambda b, h, i: (b, h)),
            pl.BlockSpec((seq, V7X_LANES), lambda b, h, i: (b, h)),
        ],
        out_specs=pl.BlockSpec((tq, V7X_LANES), lambda b, h, i: (b * nq + i, h)),
        out_shape=jax.ShapeDtypeStruct((m, dq), BF16),
        scratch_shapes=[pltpu.VMEM((2 * tq, V7X_LANES), BF16), pltpu.VMEM((2 * tq, V7X_LANES), F32),
                        pltpu.VMEM((2 * tq, V7X_LANES), F32), pltpu.VMEM((2 * tq, V7X_LANES), F32)],
        compiler_params=_params(("parallel", "parallel", "arbitrary")),
        name="attn_prompt",
    )(lam4, gs, qb, kb, vb)


def _attn_decode_kernel(pt_ref, lam_ref, gs_ref, q_ref, kn_ref, vn_ref, kc_ref, vc_ref, o_ref,
                        qm_s, m_s, l_s, acc_s, *, lam_init):
    del pt_ref
    p = pl.program_id(1)
    n_heads = q_ref.shape[0]

    @pl.when(p == 0)
    def _():
        qm_s[...] = _split_components(q_ref[...])
        m_s[...] = jnp.full_like(m_s, NEG_BIG)
        l_s[...] = jnp.zeros_like(l_s)
        acc_s[...] = jnp.zeros_like(acc_s)

    page, _, dk = kc_ref.shape
    k2 = kc_ref[...].reshape(page * n_heads, dk).astype(BF16)
    v2 = vc_ref[...].reshape(page * n_heads, vc_ref.shape[2]).astype(BF16)
    s = _dot_nt(qm_s[...].astype(BF16), k2)
    row = lax.broadcasted_iota(jnp.int32, s.shape, 0)
    col = lax.broadcasted_iota(jnp.int32, s.shape, 1)
    same_head = ((row ^ col) & (n_heads - 1)) == 0
    s = jnp.where(same_head, s, NEG_BIG)
    _softmax_update(s, v2, m_s, l_s, acc_s)

    @pl.when(p == pl.num_programs(1) - 1)
    def _():
        kn2 = jnp.concatenate([kn_ref[...], kn_ref[...]], axis=0)
        vn2 = jnp.concatenate([vn_ref[...], vn_ref[...]], axis=0)
        s_self = jnp.sum(qm_s[...] * kn2, axis=-1, keepdims=True)
        m_prev = m_s[...]
        m_new = jnp.maximum(m_prev, s_self)
        alpha = jnp.exp(m_prev - m_new)
        p_self = jnp.exp(s_self - m_new)
        l_new = alpha * l_s[...] + p_self
        acc = alpha * acc_s[...] + p_self * vn2
        o = acc / l_new
        o = o[0:n_heads, :] - _lam(lam_ref, lam_init) * o[n_heads:2 * n_heads, :]
        o_ref[...] = _subln(o, gs_ref[...], lam_init)


def _attn_decode(q3, kn3, vn3, cache_k, cache_v, page_table, lam4, gs, *, layer, lam_init):
    n_b, n_heads, dk = q3.shape
    assert n_heads & (n_heads - 1) == 0
    n_pages = page_table.shape[1]
    page = cache_k.shape[2]
    dv = cache_v.shape[4]
    row_spec = pl.BlockSpec((None, n_heads, dk), lambda b, p, pt: (b, 0, 0))
    grid_spec = pltpu.PrefetchScalarGridSpec(
        num_scalar_prefetch=1,
        grid=(n_b, n_pages),
        in_specs=[
            pl.BlockSpec(lam4.shape, lambda b, p, pt: (0, 0)),
            pl.BlockSpec((1, dv), lambda b, p, pt: (0, 0)),
            row_spec, row_spec,
            pl.BlockSpec((None, n_heads, dv), lambda b, p, pt: (b, 0, 0)),
            pl.BlockSpec((None, None, page, n_heads, dk), lambda b, p, pt: (layer, pt[b, p], 0, 0, 0)),
            pl.BlockSpec((None, None, page, n_heads, dv), lambda b, p, pt: (layer, pt[b, p], 0, 0, 0)),
        ],
        out_specs=pl.BlockSpec((None, n_heads, dv), lambda b, p, pt: (b, 0, 0)),
        scratch_shapes=[pltpu.VMEM((2 * n_heads, dk), F32), pltpu.VMEM((2 * n_heads, V7X_LANES), F32),
                        pltpu.VMEM((2 * n_heads, V7X_LANES), F32), pltpu.VMEM((2 * n_heads, dv), F32)],
    )
    return pl.pallas_call(
        functools.partial(_attn_decode_kernel, lam_init=lam_init),
        grid_spec=grid_spec,
        out_shape=jax.ShapeDtypeStruct((n_b, n_heads, dv), F32),
        compiler_params=_params(("parallel", "arbitrary")),
        name="attn_decode",
    )(page_table, lam4, gs, q3, kn3, vn3, cache_k, cache_v)


def _s5_discretize(are, aim, ldt):
    dt = jnp.exp(ldt)
    mag = jnp.exp(dt * are)
    lbr = mag * jnp.cos(dt * aim)
    lbi = mag * jnp.sin(dt * aim)
    nr = lbr - 1.0
    den = are * are + aim * aim
    cr = (nr * are + lbi * aim) / den
    ci = (lbi * are - nr * aim) / den
    return lbr, lbi, cr, ci


def _s5_input_weight(cr, ci, bre, bim):
    return jnp.concatenate([cr * bre - ci * bim, cr * bim + ci * bre], axis=1).astype(BF16)


def _s5_readout(hr, hi, cre_ref, cim_ref, d_ref, u):
    return (_dot_nt(hr.astype(BF16), cre_ref[...].astype(BF16))
            - _dot_nt(hi.astype(BF16), cim_ref[...].astype(BF16)) + d_ref[...] * u)


def _s5_prompt_kernel(are_ref, aim_ref, ldt_ref, bre_ref, bim_ref, cre_ref, cim_ref, d_ref, u_ref,
                      y_ref, hre_ref, him_ref, up_s, bur_s, bui_s, yp_s, wb_s, lam_s, car_s, *, seg):
    ch = pl.program_id(2)
    n_st = bur_s.shape[1]
    sub = V7X_SUBLANES

    @pl.when(ch == 0)
    def _():
        lbr, lbi, cr, ci = _s5_discretize(are_ref[...], aim_ref[...], ldt_ref[...])
        wb_s[...] = _s5_input_weight(cr, ci, bre_ref[...], bim_ref[...])
        lam_s[0:1, :] = lbr
        lam_s[1:2, :] = lbi
        pr, pi = lbr, lbi
        for _ in range(int(math.log2(seg))):
            pr, pi = pr * pr - pi * pi, 2.0 * pr * pi
        lam_s[2:3, :] = pr
        lam_s[3:4, :] = pi
        car_s[...] = jnp.zeros_like(car_s)

    def permute(t, c):
        up_s[pl.ds(pl.multiple_of(t * sub, sub), sub), :] = u_ref[pl.ds(t, sub, stride=seg), :]
        return c

    lax.fori_loop(0, seg, permute, 0, unroll=8)

    bu = _dot(up_s[...].astype(BF16), wb_s[...])
    bur_s[...] = bu[:, :n_st]
    bui_s[...] = bu[:, n_st:]

    lbr = jnp.broadcast_to(lam_s[0:1, :], (sub, n_st))
    lbi = jnp.broadcast_to(lam_s[1:2, :], (sub, n_st))

    def advance(t, c):
        sr, si = c
        r0 = pl.multiple_of(t * sub, sub)
        return (lbr * sr - lbi * si + bur_s[pl.ds(r0, sub), :],
                lbr * si + lbi * sr + bui_s[pl.ds(r0, sub), :])

    zero = jnp.zeros((sub, n_st), F32)
    er, ei = lax.fori_loop(0, seg, advance, (zero, zero), unroll=4)

    psr, psi = lam_s[2:3, :], lam_s[3:4, :]
    hr, hi = car_s[0:1, :], car_s[1:2, :]
    rows_r, rows_i = [], []
    for r in range(sub):
        rows_r.append(hr)
        rows_i.append(hi)
        hr, hi = (er[r:r + 1, :] + psr * hr - psi * hi, ei[r:r + 1, :] + psr * hi + psi * hr)
    car_s[0:1, :] = hr
    car_s[1:2, :] = hi

    def advance_store(t, c):
        sr, si = advance(t, c)
        r0 = pl.multiple_of(t * sub, sub)
        bur_s[pl.ds(r0, sub), :] = sr
        bui_s[pl.ds(r0, sub), :] = si
        return sr, si

    lax.fori_loop(0, seg, advance_store,
                  (jnp.concatenate(rows_r, axis=0), jnp.concatenate(rows_i, axis=0)), unroll=4)

    yp_s[...] = _s5_readout(bur_s[...], bui_s[...], cre_ref, cim_ref, d_ref, up_s[...])

    def unpermute(t, c):
        y_ref[pl.ds(t, sub, stride=seg), :] = yp_s[pl.ds(pl.multiple_of(t * sub, sub), sub), :]
        return c

    lax.fori_loop(0, seg, unpermute, 0, unroll=8)

    @pl.when(ch == pl.num_programs(2) - 1)
    def _():
        hre_ref[...] = hr
        him_ref[...] = hi


def _s5_prompt(u, lay, *, batch, seq, rows):
    m, d_ssm = u.shape
    are, aim, ldt, bre, bim, cre, cim, dsk = lay
    n_gb, n_ch_lanes, n_st = bre.shape
    n_ch = seq // rows
    seg = rows // V7X_SUBLANES
    lane_spec = pl.BlockSpec((None, 1, n_st), lambda b, g, c: (g, 0, 0))
    mat_spec = pl.BlockSpec((None, n_ch_lanes, n_st), lambda b, g, c: (g, 0, 0))
    row_spec = pl.BlockSpec((rows, n_ch_lanes), lambda b, g, c: (b * n_ch + c, g))
    st_spec = pl.BlockSpec((None, None, 1, n_st), lambda b, g, c: (b, g, 0, 0))
    st_shape = jax.ShapeDtypeStruct((batch, n_gb, 1, n_st), F32)
    return pl.pallas_call(
        functools.partial(_s5_prompt_kernel, seg=seg),
        grid=(batch, n_gb, n_ch),
        in_specs=[lane_spec, lane_spec, lane_spec, mat_spec, mat_spec, mat_spec, mat_spec,
                  pl.BlockSpec((1, n_ch_lanes), lambda b, g, c: (0, g)), row_spec],
        out_specs=[row_spec, st_spec, st_spec],
        out_shape=[jax.ShapeDtypeStruct((m, d_ssm), F32), st_shape, st_shape],
        scratch_shapes=[pltpu.VMEM((rows, n_ch_lanes), F32), pltpu.VMEM((rows, n_st), F32),
                        pltpu.VMEM((rows, n_st), F32), pltpu.VMEM((rows, n_ch_lanes), F32),
                        pltpu.VMEM((n_ch_lanes, 2 * n_st), BF16), pltpu.VMEM((4, n_st), F32),
                        pltpu.VMEM((2, n_st), F32)],
        compiler_params=_params(("parallel", "parallel", "arbitrary")),
        name="s5_prompt",
    )(are, aim, ldt, bre, bim, cre, cim, dsk, u)


def _s5_step_kernel(are_ref, aim_ref, ldt_ref, bre_ref, bim_ref, cre_ref, cim_ref, d_ref, u_ref,
                    h0r_ref, h0i_ref, y_ref, hre_ref, him_ref):
    n_st = h0r_ref.shape[1]
    lbr, lbi, cr, ci = _s5_discretize(are_ref[...], aim_ref[...], ldt_ref[...])
    u = u_ref[...]
    bu = _dot(u.astype(BF16), _s5_input_weight(cr, ci, bre_ref[...], bim_ref[...]))
    h0r, h0i = h0r_ref[...], h0i_ref[...]
    hr = lbr * h0r - lbi * h0i + bu[:, :n_st]
    hi = lbr * h0i + lbi * h0r + bu[:, n_st:]
    hre_ref[...] = hr
    him_ref[...] = hi
    y_ref[...] = _s5_readout(hr, hi, cre_ref, cim_ref, d_ref, u)


def _s5_step(u, h0r, h0i, lay):
    m, d_ssm = u.shape
    are, aim, ldt, bre, bim, cre, cim, dsk = lay
    n_gb, n_ch_lanes, n_st = bre.shape
    lane_spec = pl.BlockSpec((None, 1, n_st), lambda g: (g, 0, 0))
    mat_spec = pl.BlockSpec((None, n_ch_lanes, n_st), lambda g: (g, 0, 0))
    row_spec = pl.BlockSpec((m, n_ch_lanes), lambda g: (0, g))
    st_spec = pl.BlockSpec((m, n_st), lambda g: (0, g))
    st_shape = jax.ShapeDtypeStruct(h0r.shape, F32)
    return pl.pallas_call(
        _s5_step_kernel,
        grid=(n_gb,),
        in_specs=[lane_spec, lane_spec, lane_spec, mat_spec, mat_spec, mat_spec, mat_spec,
                  pl.BlockSpec((1, n_ch_lanes), lambda g: (0, g)), row_spec, st_spec, st_spec],
        out_specs=[row_spec, st_spec, st_spec],
        out_shape=[jax.ShapeDtypeStruct((m, d_ssm), F32), st_shape, st_shape],
        compiler_params=_params(("parallel",)),
        name="s5_step",
    )(are, aim, ldt, bre, bim, cre, cim, dsk, u, h0r, h0i)


def _s5_layout(a_re, a_im, log_dt, b_re, b_im, c_re, c_im, d_skip):
    n_g, n_st = a_re.shape
    gl = SSM_GROUP_BLOCK
    n_gb = n_g // gl
    same = jnp.eye(gl, dtype=jnp.bool_)[None, :, None, :, None]

    def lanes(a):
        return a.reshape(n_gb, 1, gl * n_st)

    def block_diag(w):
        w5 = w.reshape(n_gb, gl, SSM_GROUP, 1, n_st)
        return jnp.where(same, w5, 0.0).reshape(n_gb, gl * SSM_GROUP, gl * n_st)

    return (lanes(a_re), lanes(a_im), lanes(jnp.broadcast_to(log_dt[:, None], (n_g, n_st))),
            block_diag(b_re.transpose(0, 2, 1)), block_diag(b_im.transpose(0, 2, 1)),
            block_diag(c_re), block_diag(c_im), d_skip.reshape(1, n_g * SSM_GROUP))


def _mix_kernel(x_ref, ys_ref, o_ref, ga_ref, gb_ref, wglu_ref, wso_ref, wao_ref, wo_ref, out_ref):
    ys = ys_ref[...]
    ya = 0.5 * ys * (1.0 + lax.erf(ys * (2.0 ** -0.5)))
    glu = (ya * jax.nn.sigmoid(_dot(ya.astype(BF16), wglu_ref[...]))).astype(BF16)
    ssm = _dot(glu, wso_ref[...])
    att = _dot(o_ref[...], wao_ref[...])
    mix = (ga_ref[...].astype(F32) * ssm + gb_ref[...].astype(F32) * att).astype(BF16)
    out_ref[...] = x_ref[...] + _dot(mix, wo_ref[...])


def _mix(x, ys, o, gates, w_glu, w_ssm_out, w_attn_out, w_o, *, tm):
    m, d = x.shape
    d_ssm = ys.shape[1]
    d_att = o.shape[1]

    def resident(w):
        return pl.BlockSpec(w.shape, lambda i: (0, 0), pipeline_mode=pl.Buffered(1))

    return pl.pallas_call(
        _mix_kernel,
        grid=(m // tm,),
        in_specs=[
            pl.BlockSpec((tm, d), lambda i: (i, 0)),
            pl.BlockSpec((tm, d_ssm), lambda i: (i, 0)),
            pl.BlockSpec((tm, d_att), lambda i: (i, 0)),
            pl.BlockSpec((tm, d), lambda i: (i, 0)),
            pl.BlockSpec((tm, d), lambda i: (i, 1)),
            resident(w_glu), resident(w_ssm_out), resident(w_attn_out), resident(w_o),
        ],
        out_specs=pl.BlockSpec((tm, d), lambda i: (i, 0)),
        out_shape=jax.ShapeDtypeStruct((m, d), F32),
        compiler_params=_params(("parallel",)),
        name="mix",
    )(x, ys, o, gates, gates, w_glu, w_ssm_out, w_attn_out, w_o)


def _rotary_tables(pos, rows):
    half = ROT_DIM // 2
    inv = ROPE_THETA ** (-jnp.arange(half, dtype=F32) / half)
    ang = pos.astype(F32)[:, None] * inv[None, :]
    cos, sin = jnp.cos(ang), jnp.sin(ang)
    t = pos.shape[0]
    zh = jnp.zeros((t, half), F32)
    rest0 = jnp.zeros((t, HEAD_DIM - ROT_DIM), F32)
    cos_c = jnp.concatenate([cos, cos, jnp.ones((t, HEAD_DIM - ROT_DIM), F32)], axis=1)
    s1_c = jnp.concatenate([-sin, zh, rest0], axis=1)
    s2_c = jnp.concatenate([zh, sin, rest0], axis=1)
    return tuple(jnp.broadcast_to(jnp.tile(a, (1, 2)), (rows, 2 * HEAD_DIM)) for a in (cos_c, s1_c, s2_c))


def kernel(x_prompt, x_sample, cache_k, cache_v, state_ssm_re, state_ssm_im, page_table, norm_ffn1, ffn1_w1, ffn1_w3, ffn1_w2, norm_mix, w_in, lam_q1, lam_k1, lam_q2, lam_k2, g_subln, w_attn_out, ssm_a_re, ssm_a_im, ssm_log_dt, ssm_b_re, ssm_b_im, ssm_c_re, ssm_c_im, ssm_d, w_glu, w_ssm_out, w_o, norm_ffn2, ffn2_w1, ffn2_w3, ffn2_w2, norm_final):
    batch, seq, d = x_prompt.shape
    n_dec, dec_seq, _ = x_sample.shape
    assert dec_seq == 1
    depth = cache_k.shape[0]
    n_heads, d_k = cache_k.shape[3], cache_k.shape[4]
    d_v = cache_v.shape[4]
    n_groups, n_state = ssm_a_re.shape[1], ssm_a_re.shape[2]
    d_qk, d_att, d_ssm = n_heads * d_k, n_heads * d_v, n_groups * SSM_GROUP
    past_len = page_table.shape[1] * cache_k.shape[2]
    m_p = batch * seq

    tabs_p = _rotary_tables(jnp.arange(seq, dtype=jnp.int32), seq)
    tabs_s = _rotary_tables(past_len + jnp.arange(dec_seq, dtype=jnp.int32), n_dec)
    row = lambda a: a.reshape(1, -1)
    gfin = row(norm_final)

    h_p = x_prompt.reshape(m_p, d)
    h_s = x_sample.reshape(n_dec, d)
    outs = [[] for _ in range(8)]
    for l in range(depth):
        lam_init = 0.8 - 0.6 * math.exp(-0.3 * l)
        last = l == depth - 1
        w11, w13, w12 = ffn1_w1[l].astype(BF16), ffn1_w3[l].astype(BF16), ffn1_w2[l].astype(BF16)
        w21, w23, w22 = ffn2_w1[l].astype(BF16), ffn2_w3[l].astype(BF16), ffn2_w2[l].astype(BF16)
        win = w_in[l].astype(BF16)
        wglu, wso = w_glu[l].astype(BF16), w_ssm_out[l].astype(BF16)
        wao, wo = w_attn_out[l].astype(BF16), w_o[l].astype(BF16)
        lam4 = jnp.stack([lam_q1[l], lam_k1[l], lam_q2[l], lam_k2[l]])
        gs = row(g_subln[l])
        lay = _s5_layout(ssm_a_re[l], ssm_a_im[l], ssm_log_dt[l], ssm_b_re[l], ssm_b_im[l],
                         ssm_c_re[l], ssm_c_im[l], ssm_d[l])

        def half1(x, tm):
            return _ffn(x, row(norm_ffn1[l]), w11, w13, w12, gfin, final_norm=False, tm=tm, tf=512)

        def half2(x, tm):
            return _ffn(x, row(norm_ffn2[l]), w21, w23, w22, gfin, final_norm=last, tm=tm, tf=512)

        x1 = half1(h_p, 512)
        q, k, kb, v, vb, u, gates = _project(x1, row(norm_mix[l]), win, *tabs_p, tm=512,
                                             d_qk=d_qk, d_v=d_att, d_ssm=d_ssm)
        o = _attn_prompt(q, kb, vb, lam4, gs, batch=batch, seq=seq, tq=512, lam_init=lam_init)
        ys, hre, him = _s5_prompt(u, lay, batch=batch, seq=seq, rows=1024)
        x2 = _mix(x1, ys, o, gates, wglu, wso, wao, wo, tm=256)
        h_p = half2(x2, 512)
        outs[0].append(k.reshape(batch, seq, n_heads, d_k))
        outs[1].append(v.reshape(batch, seq, n_heads, d_v))
        outs[2].append(hre.reshape(batch, n_groups, n_state))
        outs[3].append(him.reshape(batch, n_groups, n_state))

        x1 = half1(h_s, n_dec)
        q, k, kb, v, vb, u, gates = _project(x1, row(norm_mix[l]), win, *tabs_s, tm=n_dec,
                                             d_qk=d_qk, d_v=d_att, d_ssm=d_ssm)
        o = _attn_decode(q.astype(F32).reshape(n_dec, n_heads, d_k), k.reshape(n_dec, n_heads, d_k),
                         v.reshape(n_dec, n_heads, d_v), cache_k, cache_v, page_table, lam4, gs,
                         layer=l, lam_init=lam_init)
        ys, hre, him = _s5_step(u, state_ssm_re[l].reshape(n_dec, n_groups * n_state),
                                state_ssm_im[l].reshape(n_dec, n_groups * n_state), lay)
        x2 = _mix(x1, ys, o.reshape(n_dec, d_att).astype(BF16), gates, wglu, wso, wao, wo, tm=n_dec)
        h_s = half2(x2, n_dec)
        outs[4].append(k.reshape(n_dec, dec_seq, n_heads, d_k))
        outs[5].append(v.reshape(n_dec, dec_seq, n_heads, d_v))
        outs[6].append(hre.reshape(n_dec, n_groups, n_state))
        outs[7].append(him.reshape(n_dec, n_groups, n_state))

    return (h_p.reshape(batch, seq, d), h_s.reshape(n_dec, dec_seq, d)) + tuple(jnp.stack(o) for o in outs)
```

```python
import functools
import math

import jax
import jax.numpy as jnp
from jax import lax
from jax.experimental import pallas as pl
from jax.experimental.pallas import tpu as pltpu

F32 = jnp.float32
BF16 = jnp.bfloat16

HEAD_DIM = 64
ROT_DIM = HEAD_DIM // 4
ROPE_THETA = 500000.0
EPS = 1e-6
SSM_GROUP = 16
SSM_GROUP_BLOCK = 8

V7X_LANES = 128
V7X_SUBLANES = 8
V7X_MXU_WIDTH = 256
MXU_WIDTH = V7X_MXU_WIDTH
V7X_VMEM_BYTES = 64 * 1024 * 1024
VMEM_LIMIT = V7X_VMEM_BYTES * 3 // 4

NEG_BIG = -1e30


def _dot(a, b):
    return jnp.dot(a, b, preferred_element_type=F32)


def _dot_nt(a, b):
    return lax.dot_general(a, b, (((1,), (1,)), ((), ())), preferred_element_type=F32)


def _rms(x, g):
    return x * lax.rsqrt(jnp.mean(x * x, axis=-1, keepdims=True) + EPS) * g


def _params(sem):
    return pltpu.CompilerParams(dimension_semantics=sem, vmem_limit_bytes=VMEM_LIMIT)


def _ffn_kernel(x_ref, g_ref, w1_ref, w3_ref, w2_ref, gf_ref, o_ref, xn_s, acc_s, *, final_norm):
    f = pl.program_id(1)

    @pl.when(f == 0)
    def _():
        xn_s[...] = _rms(x_ref[...], g_ref[...]).astype(BF16)
        acc_s[...] = jnp.zeros_like(acc_s)

    xn = xn_s[...]
    acc = acc_s[...]
    for c in range(w1_ref.shape[1] // MXU_WIDTH):
        cols = slice(c * MXU_WIDTH, (c + 1) * MXU_WIDTH)
        h1 = _dot(xn, w1_ref[:, cols])
        h3 = _dot(xn, w3_ref[:, cols])
        hh = (h1 * jax.nn.sigmoid(h1) * h3).astype(BF16)
        acc = acc + _dot(hh, w2_ref[cols, :])
    acc_s[...] = acc

    @pl.when(f == pl.num_programs(1) - 1)
    def _():
        y = x_ref[...] + 0.5 * acc_s[...]
        if final_norm:
            y = _rms(y, gf_ref[...])
        o_ref[...] = y


def _ffn(x, g, w1, w3, w2, gf, *, final_norm, tm, tf):
    m, d = x.shape
    dff = w1.shape[1]
    return pl.pallas_call(
        functools.partial(_ffn_kernel, final_norm=final_norm),
        grid=(m // tm, dff // tf),
        in_specs=[
            pl.BlockSpec((tm, d), lambda i, f: (i, 0)),
            pl.BlockSpec((1, d), lambda i, f: (0, 0)),
            pl.BlockSpec((d, tf), lambda i, f: (0, f)),
            pl.BlockSpec((d, tf), lambda i, f: (0, f)),
            pl.BlockSpec((tf, d), lambda i, f: (f, 0)),
            pl.BlockSpec((1, d), lambda i, f: (0, 0)),
        ],
        out_specs=pl.BlockSpec((tm, d), lambda i, f: (i, 0)),
        out_shape=jax.ShapeDtypeStruct((m, d), F32),
        scratch_shapes=[pltpu.VMEM((tm, d), BF16), pltpu.VMEM((tm, d), F32)],
        compiler_params=_params(("parallel", "arbitrary")),
        name="ffn",
    )(x, g, w1, w3, w2, gf)


def _rotate(a, cos_ref, s1_ref, s2_ref):
    cos, s1, s2 = cos_ref[...], s1_ref[...], s2_ref[...]
    half = ROT_DIM // 2
    out = []
    for h in range(a.shape[1] // V7X_LANES):
        blk = a[:, h * V7X_LANES:(h + 1) * V7X_LANES]
        up = pltpu.roll(blk, V7X_LANES - half, 1)
        dn = pltpu.roll(blk, half, 1)
        out.append(blk * cos + up * s1 + dn * s2)
    return jnp.concatenate(out, axis=1)


def _col_chunks(n):
    return [slice(c, c + MXU_WIDTH) for c in range(0, n, MXU_WIDTH)]


def _proj_qk_kernel(x_ref, g_ref, w_ref, cos_ref, s1_ref, s2_ref, q_ref, k_ref, kb_ref, xn_s):
    xn_s[...] = _rms(x_ref[...], g_ref[...]).astype(BF16)
    n = q_ref.shape[1]
    for cols in _col_chunks(n):
        r = _rotate(_dot(xn_s[...], w_ref[:, cols]), cos_ref, s1_ref, s2_ref)
        q_ref[:, cols] = (r * (HEAD_DIM ** -0.5)).astype(BF16)
    for cols in _col_chunks(n):
        r = _rotate(_dot(xn_s[...], w_ref[:, slice(n + cols.start, n + cols.stop)]), cos_ref, s1_ref, s2_ref)
        k_ref[:, cols] = r
        kb_ref[:, cols] = r.astype(BF16)


def _proj_vu_kernel(x_ref, g_ref, w_ref, v_ref, vb_ref, u_ref, xn_s):
    xn_s[...] = _rms(x_ref[...], g_ref[...]).astype(BF16)
    n = v_ref.shape[1]
    for cols in _col_chunks(n):
        a = _dot(xn_s[...], w_ref[:, cols])
        v_ref[:, cols] = a
        vb_ref[:, cols] = a.astype(BF16)
    for cols in _col_chunks(u_ref.shape[1]):
        u_ref[:, cols] = _dot(xn_s[...], w_ref[:, slice(n + cols.start, n + cols.stop)])


def _proj_gate_kernel(x_ref, g_ref, w_ref, s_ref, xn_s):
    @pl.when(pl.program_id(1) == 0)
    def _():
        xn_s[...] = _rms(x_ref[...], g_ref[...]).astype(BF16)

    for cols in _col_chunks(s_ref.shape[1]):
        s_ref[:, cols] = jax.nn.sigmoid(_dot(xn_s[...], w_ref[:, cols])).astype(BF16)


def _project(x, g, w_in, cos_t, s1_t, s2_t, *, tm, d_qk, d_v, d_ssm):
    m, d = x.shape
    tn = 2 * d_qk
    assert d_v + d_ssm == tn and (w_in.shape[1] - 2 * tn) % tn == 0
    n_gate = (w_in.shape[1] - 2 * tn) // tn
    n_t = cos_t.shape[0] // tm
    x_spec = pl.BlockSpec((tm, d), lambda i, j: (i, 0))
    g_spec = pl.BlockSpec((1, d), lambda i, j: (0, 0))
    tab_spec = pl.BlockSpec((tm, V7X_LANES), lambda i, j: (i % n_t, 0))
    xn_scratch = [pltpu.VMEM((tm, d), BF16)]

    def row_spec(n):
        return pl.BlockSpec((tm, n), lambda i, j: (i, 0))

    def slab_spec(first):
        return pl.BlockSpec((d, tn), lambda i, j: (0, first + j))

    q, k, kb = pl.pallas_call(
        _proj_qk_kernel,
        grid=(m // tm, 1),
        in_specs=[x_spec, g_spec, slab_spec(0), tab_spec, tab_spec, tab_spec],
        out_specs=[row_spec(d_qk)] * 3,
        out_shape=[jax.ShapeDtypeStruct((m, d_qk), BF16), jax.ShapeDtypeStruct((m, d_qk), F32),
                   jax.ShapeDtypeStruct((m, d_qk), BF16)],
        scratch_shapes=xn_scratch,
        compiler_params=_params(("parallel", "arbitrary")),
        name="proj_qk",
    )(x, g, w_in, cos_t, s1_t, s2_t)

    v, vb, u = pl.pallas_call(
        _proj_vu_kernel,
        grid=(m // tm, 1),
        in_specs=[x_spec, g_spec, slab_spec(1)],
        out_specs=[row_spec(d_v), row_spec(d_v), row_spec(d_ssm)],
        out_shape=[jax.ShapeDtypeStruct((m, d_v), F32), jax.ShapeDtypeStruct((m, d_v), BF16),
                   jax.ShapeDtypeStruct((m, d_ssm), F32)],
        scratch_shapes=xn_scratch,
        compiler_params=_params(("parallel", "arbitrary")),
        name="proj_vu",
    )(x, g, w_in)

    gates = pl.pallas_call(
        _proj_gate_kernel,
        grid=(m // tm, n_gate),
        in_specs=[x_spec, g_spec, slab_spec(2)],
        out_specs=pl.BlockSpec((tm, tn), lambda i, j: (i, j)),
        out_shape=jax.ShapeDtypeStruct((m, n_gate * tn), BF16),
        scratch_shapes=xn_scratch,
        compiler_params=_params(("parallel", "arbitrary")),
        name="proj_gate",
    )(x, g, w_in)
    return q, k, kb, v, vb, u, gates


def _lam(lam_ref, lam_init):
    a = jnp.sum(lam_ref[0:1, :] * lam_ref[1:2, :], axis=-1, keepdims=True)
    b = jnp.sum(lam_ref[2:3, :] * lam_ref[3:4, :], axis=-1, keepdims=True)
    return jnp.exp(a) - jnp.exp(b) + lam_init


def _subln(o, gs, lam_init):
    return _rms(o, gs) * (1.0 - lam_init)


def _split_components(q):
    lane = lax.broadcasted_iota(jnp.int32, q.shape, 1)
    zero = jnp.zeros_like(q)
    return jnp.concatenate([jnp.where(lane < HEAD_DIM, q, zero), jnp.where(lane >= HEAD_DIM, q, zero)], axis=0)


def _softmax_update(s, v, m_s, l_s, acc_s):
    m_prev = m_s[...]
    m_new = jnp.maximum(m_prev, jnp.max(s, axis=-1, keepdims=True))
    alpha = jnp.exp(m_prev - m_new)
    p = jnp.exp(s - jnp.tile(m_new, (1, s.shape[1] // V7X_LANES)))
    l_s[...] = alpha * l_s[...] + jnp.sum(p, axis=-1, keepdims=True)
    acc_s[...] = alpha * acc_s[...] + _dot(p.astype(BF16), v)
    m_s[...] = m_new


def _attn_prompt_kernel(lam_ref, gs_ref, q_ref, k_ref, v_ref, o_ref, qq_s, m_s, l_s, acc_s, *, tq, lam_init):
    i = pl.program_id(2)
    qq_s[...] = _split_components(q_ref[...])
    m_s[...] = jnp.full_like(m_s, NEG_BIG)
    l_s[...] = jnp.zeros_like(l_s)
    acc_s[...] = jnp.zeros_like(acc_s)

    def step(j, masked):
        start = pl.multiple_of(j * tq, tq)
        kt = k_ref[pl.ds(start, tq), :]
        vt = v_ref[pl.ds(start, tq), :]
        s = _dot_nt(qq_s[...], kt)
        if masked:
            row = lax.broadcasted_iota(jnp.int32, s.shape, 0)
            row = jnp.where(row >= tq, row - tq, row)
            col = lax.broadcasted_iota(jnp.int32, s.shape, 1)
            s = jnp.where(col <= row, s, NEG_BIG)
        _softmax_update(s, vt, m_s, l_s, acc_s)

    def body(j, c):
        step(j, False)
        return c

    lax.fori_loop(0, i, body, 0)
    step(i, True)

    o1 = acc_s[0:tq, :] / l_s[0:tq, :]
    o2 = acc_s[tq:2 * tq, :] / l_s[tq:2 * tq, :]
    o = o1 - _lam(lam_ref, lam_init) * o2
    o_ref[...] = _subln(o, gs_ref[...], lam_init).astype(BF16)


def _attn_prompt(qb, kb, vb, lam4, gs, *, batch, seq, tq, lam_init):
    m, dq = qb.shape
    n_heads = dq // V7X_LANES
    nq = seq // tq
    return pl.pallas_call(
        functools.partial(_attn_prompt_kernel, tq=tq, lam_init=lam_init),
        grid=(batch, n_heads, nq),
        in_specs=[
            pl.BlockSpec(lam4.shape, lambda b, h, i: (0, 0)),
            pl.BlockSpec((1, V7X_LANES), lambda b, h, i: (0, 0)),
            pl.BlockSpec((tq, V7X_LANES), lambda b, h, i: (b * nq + i, h)),
            pl.BlockSpec((seq, V7X_LANES), lambda b, h, i: (b, h)),
            pl.BlockSpec((seq, V7X_LANES), lambda b, h, i: (b, h)),
        ],
        out_specs=pl.BlockSpec((tq, V7X_LANES), lambda b, h, i: (b * nq + i, h)),
        out_shape=jax.ShapeDtypeStruct((m, dq), BF16),
        scratch_shapes=[pltpu.VMEM((2 * tq, V7X_LANES), BF16), pltpu.VMEM((2 * tq, V7X_LANES), F32),
                        pltpu.VMEM((2 * tq, V7X_LANES), F32), pltpu.VMEM((2 * tq, V7X_LANES), F32)],
        compiler_params=_params(("parallel", "parallel", "arbitrary")),
        name="attn_prompt",
    )(lam4, gs, qb, kb, vb)


def _attn_decode_kernel(pt_ref, lam_ref, gs_ref, q_ref, kn_ref, vn_ref, *refs, n_pages, lam_init):
    del pt_ref
    kc_refs, vc_refs, o_ref = refs[:n_pages], refs[n_pages:2 * n_pages], refs[2 * n_pages]
    n_heads = q_ref.shape[0]
    page, _, dk = kc_refs[0].shape
    dv = vc_refs[0].shape[2]

    qm = _split_components(q_ref[...])
    qmb = qm.astype(BF16)
    shape = (2 * n_heads, page * n_heads)
    row = lax.broadcasted_iota(jnp.int32, shape, 0)
    col = lax.broadcasted_iota(jnp.int32, shape, 1)
    same_head = ((row ^ col) & (n_heads - 1)) == 0

    kn2 = jnp.concatenate([kn_ref[...], kn_ref[...]], axis=0)
    vn2 = jnp.concatenate([vn_ref[...], vn_ref[...]], axis=0)
    s_self = jnp.sum(qm * kn2, axis=-1, keepdims=True)

    scores = []
    m = s_self
    for kc_ref in kc_refs:
        k2 = kc_ref[...].reshape(page * n_heads, dk).astype(BF16)
        s = jnp.where(same_head, _dot_nt(qmb, k2), NEG_BIG)
        scores.append(s)
        m = jnp.maximum(m, jnp.max(s, axis=-1, keepdims=True))

    p_self = jnp.exp(s_self - m)
    l = p_self
    acc = p_self * vn2
    for s, vc_ref in zip(scores, vc_refs):
        p = jnp.exp(s - m)
        l = l + jnp.sum(p, axis=-1, keepdims=True)
        acc = acc + _dot(p.astype(BF16), vc_ref[...].reshape(page * n_heads, dv).astype(BF16))

    o = acc / l
    o = o[0:n_heads, :] - _lam(lam_ref, lam_init) * o[n_heads:2 * n_heads, :]
    o_ref[...] = _subln(o, gs_ref[...], lam_init)


def _attn_decode(q3, kn3, vn3, cache_k, cache_v, page_table, lam4, gs, *, layer, lam_init):
    n_b, n_heads, dk = q3.shape
    assert n_heads & (n_heads - 1) == 0
    n_pages = page_table.shape[1]
    page = cache_k.shape[2]
    dv = cache_v.shape[4]
    row_spec = pl.BlockSpec((None, n_heads, dk), lambda b, pt: (b, 0, 0))

    def page_specs(d):
        return [pl.BlockSpec((None, None, page, n_heads, d), lambda b, pt, p=p: (layer, pt[b, p], 0, 0, 0))
                for p in range(n_pages)]

    grid_spec = pltpu.PrefetchScalarGridSpec(
        num_scalar_prefetch=1,
        grid=(n_b,),
        in_specs=[
            pl.BlockSpec(lam4.shape, lambda b, pt: (0, 0)),
            pl.BlockSpec((1, dv), lambda b, pt: (0, 0)),
            row_spec, row_spec,
            pl.BlockSpec((None, n_heads, dv), lambda b, pt: (b, 0, 0)),
            *page_specs(dk), *page_specs(dv),
        ],
        out_specs=pl.BlockSpec((None, n_heads, dv), lambda b, pt: (b, 0, 0)),
    )
    return pl.pallas_call(
        functools.partial(_attn_decode_kernel, n_pages=n_pages, lam_init=lam_init),
        grid_spec=grid_spec,
        out_shape=jax.ShapeDtypeStruct((n_b, n_heads, dv), F32),
        compiler_params=_params(("parallel",)),
        name="attn_decode",
    )(page_table, lam4, gs, q3, kn3, vn3, *([cache_k] * n_pages), *([cache_v] * n_pages))


def _s5_discretize(are, aim, ldt):
    dt = jnp.exp(ldt)
    mag = jnp.exp(dt * are)
    lbr = mag * jnp.cos(dt * aim)
    lbi = mag * jnp.sin(dt * aim)
    nr = lbr - 1.0
    den = are * are + aim * aim
    cr = (nr * are + lbi * aim) / den
    ci = (lbi * are - nr * aim) / den
    return lbr, lbi, cr, ci


def _s5_input_weight(cr, ci, bre, bim):
    return jnp.concatenate([cr * bre - ci * bim, cr * bim + ci * bre], axis=1).astype(BF16)


def _s5_readout(hr, hi, cre_ref, cim_ref, d_ref, u):
    return (_dot_nt(hr.astype(BF16), cre_ref[...].astype(BF16))
            - _dot_nt(hi.astype(BF16), cim_ref[...].astype(BF16)) + d_ref[...] * u)


def _s5_prompt_kernel(are_ref, aim_ref, ldt_ref, bre_ref, bim_ref, cre_ref, cim_ref, d_ref, u_ref,
                      y_ref, hre_ref, him_ref, up_s, bur_s, bui_s, yp_s, wb_s, lam_s, car_s, *, seg):
    ch = pl.program_id(2)
    n_st = bur_s.shape[1]
    sub = V7X_SUBLANES

    @pl.when(ch == 0)
    def _():
        lbr, lbi, cr, ci = _s5_discretize(are_ref[...], aim_ref[...], ldt_ref[...])
        wb_s[...] = _s5_input_weight(cr, ci, bre_ref[...], bim_ref[...])
        lam_s[0:1, :] = lbr
        lam_s[1:2, :] = lbi
        pr, pi = lbr, lbi
        for _ in range(int(math.log2(seg))):
            pr, pi = pr * pr - pi * pi, 2.0 * pr * pi
        lam_s[2:3, :] = pr
        lam_s[3:4, :] = pi
        car_s[...] = jnp.zeros_like(car_s)

    def permute(t, c):
        up_s[pl.ds(pl.multiple_of(t * sub, sub), sub), :] = u_ref[pl.ds(t, sub, stride=seg), :]
        return c

    lax.fori_loop(0, seg, permute, 0, unroll=8)

    bu = _dot(up_s[...].astype(BF16), wb_s[...])
    bur_s[...] = bu[:, :n_st]
    bui_s[...] = bu[:, n_st:]

    lbr = jnp.broadcast_to(lam_s[0:1, :], (sub, n_st))
    lbi = jnp.broadcast_to(lam_s[1:2, :], (sub, n_st))

    def advance(t, c):
        sr, si = c
        r0 = pl.multiple_of(t * sub, sub)
        return (lbr * sr - lbi * si + bur_s[pl.ds(r0, sub), :],
                lbr * si + lbi * sr + bui_s[pl.ds(r0, sub), :])

    zero = jnp.zeros((sub, n_st), F32)
    er, ei = lax.fori_loop(0, seg, advance, (zero, zero), unroll=4)

    psr, psi = lam_s[2:3, :], lam_s[3:4, :]
    hr, hi = car_s[0:1, :], car_s[1:2, :]
    rows_r, rows_i = [], []
    for r in range(sub):
        rows_r.append(hr)
        rows_i.append(hi)
        hr, hi = (er[r:r + 1, :] + psr * hr - psi * hi, ei[r:r + 1, :] + psr * hi + psi * hr)
    car_s[0:1, :] = hr
    car_s[1:2, :] = hi

    def advance_store(t, c):
        sr, si = advance(t, c)
        r0 = pl.multiple_of(t * sub, sub)
        bur_s[pl.ds(r0, sub), :] = sr
        bui_s[pl.ds(r0, sub), :] = si
        return sr, si

    lax.fori_loop(0, seg, advance_store,
                  (jnp.concatenate(rows_r, axis=0), jnp.concatenate(rows_i, axis=0)), unroll=4)

    yp_s[...] = _s5_readout(bur_s[...], bui_s[...], cre_ref, cim_ref, d_ref, up_s[...])

    def unpermute(t, c):
        y_ref[pl.ds(t, sub, stride=seg), :] = yp_s[pl.ds(pl.multiple_of(t * sub, sub), sub), :]
        return c

    lax.fori_loop(0, seg, unpermute, 0, unroll=8)

    @pl.when(ch == pl.num_programs(2) - 1)
    def _():
        hre_ref[...] = hr
        him_ref[...] = hi


def _s5_prompt(u, lay, *, batch, seq, rows):
    m, d_ssm = u.shape
    are, aim, ldt, bre, bim, cre, cim, dsk = lay
    n_gb, n_ch_lanes, n_st = bre.shape
    n_ch = seq // rows
    seg = rows // V7X_SUBLANES
    lane_spec = pl.BlockSpec((None, 1, n_st), lambda b, g, c: (g, 0, 0))
    mat_spec = pl.BlockSpec((None, n_ch_lanes, n_st), lambda b, g, c: (g, 0, 0))
    row_spec = pl.BlockSpec((rows, n_ch_lanes), lambda b, g, c: (b * n_ch + c, g))
    st_spec = pl.BlockSpec((None, None, 1, n_st), lambda b, g, c: (b, g, 0, 0))
    st_shape = jax.ShapeDtypeStruct((batch, n_gb, 1, n_st), F32)
    return pl.pallas_call(
        functools.partial(_s5_prompt_kernel, seg=seg),
        grid=(batch, n_gb, n_ch),
        in_specs=[lane_spec, lane_spec, lane_spec, mat_spec, mat_spec, mat_spec, mat_spec,
                  pl.BlockSpec((1, n_ch_lanes), lambda b, g, c: (0, g)), row_spec],
        out_specs=[row_spec, st_spec, st_spec],
        out_shape=[jax.ShapeDtypeStruct((m, d_ssm), F32), st_shape, st_shape],
        scratch_shapes=[pltpu.VMEM((rows, n_ch_lanes), F32), pltpu.VMEM((rows, n_st), F32),
                        pltpu.VMEM((rows, n_st), F32), pltpu.VMEM((rows, n_ch_lanes), F32),
                        pltpu.VMEM((n_ch_lanes, 2 * n_st), BF16), pltpu.VMEM((4, n_st), F32),
                        pltpu.VMEM((2, n_st), F32)],
        compiler_params=_params(("parallel", "parallel", "arbitrary")),
        name="s5_prompt",
    )(are, aim, ldt, bre, bim, cre, cim, dsk, u)


def _s5_step_kernel(are_ref, aim_ref, ldt_ref, bre_ref, bim_ref, cre_ref, cim_ref, d_ref, u_ref,
                    h0r_ref, h0i_ref, y_ref, hre_ref, him_ref):
    n_st = h0r_ref.shape[1]
    lbr, lbi, cr, ci = _s5_discretize(are_ref[...], aim_ref[...], ldt_ref[...])
    u = u_ref[...]
    bu = _dot(u.astype(BF16), _s5_input_weight(cr, ci, bre_ref[...], bim_ref[...]))
    h0r, h0i = h0r_ref[...], h0i_ref[...]
    hr = lbr * h0r - lbi * h0i + bu[:, :n_st]
    hi = lbr * h0i + lbi * h0r + bu[:, n_st:]
    hre_ref[...] = hr
    him_ref[...] = hi
    y_ref[...] = _s5_readout(hr, hi, cre_ref, cim_ref, d_ref, u)


def _s5_step(u, h0r, h0i, lay):
    m, d_ssm = u.shape
    are, aim, ldt, bre, bim, cre, cim, dsk = lay
    n_gb, n_ch_lanes, n_st = bre.shape
    lane_spec = pl.BlockSpec((None, 1, n_st), lambda g: (g, 0, 0))
    mat_spec = pl.BlockSpec((None, n_ch_lanes, n_st), lambda g: (g, 0, 0))
    row_spec = pl.BlockSpec((m, n_ch_lanes), lambda g: (0, g))
    st_spec = pl.BlockSpec((m, n_st), lambda g: (0, g))
    st_shape = jax.ShapeDtypeStruct(h0r.shape, F32)
    return pl.pallas_call(
        _s5_step_kernel,
        grid=(n_gb,),
        in_specs=[lane_spec, lane_spec, lane_spec, mat_spec, mat_spec, mat_spec, mat_spec,
                  pl.BlockSpec((1, n_ch_lanes), lambda g: (0, g)), row_spec, st_spec, st_spec],
        out_specs=[row_spec, st_spec, st_spec],
        out_shape=[jax.ShapeDtypeStruct((m, d_ssm), F32), st_shape, st_shape],
        compiler_params=_params(("parallel",)),
        name="s5_step",
    )(are, aim, ldt, bre, bim, cre, cim, dsk, u, h0r, h0i)


def _s5_layout(a_re, a_im, log_dt, b_re, b_im, c_re, c_im, d_skip):
    n_g, n_st = a_re.shape
    gl = SSM_GROUP_BLOCK
    n_gb = n_g // gl
    same = jnp.eye(gl, dtype=jnp.bool_)[None, :, None, :, None]

    def lanes(a):
        return a.reshape(n_gb, 1, gl * n_st)

    def block_diag(w):
        w5 = w.reshape(n_gb, gl, SSM_GROUP, 1, n_st)
        return jnp.where(same, w5, 0.0).reshape(n_gb, gl * SSM_GROUP, gl * n_st)

    return (lanes(a_re), lanes(a_im), lanes(jnp.broadcast_to(log_dt[:, None], (n_g, n_st))),
            block_diag(b_re.transpose(0, 2, 1)), block_diag(b_im.transpose(0, 2, 1)),
            block_diag(c_re), block_diag(c_im), d_skip.reshape(1, n_g * SSM_GROUP))


def _mix_kernel(x_ref, ys_ref, o_ref, ga_ref, gb_ref, wglu_ref, wso_ref, wao_ref, wo_ref, out_ref):
    ys = ys_ref[...]
    ya = 0.5 * ys * (1.0 + lax.erf(ys * (2.0 ** -0.5)))
    glu = (ya * jax.nn.sigmoid(_dot(ya.astype(BF16), wglu_ref[...]))).astype(BF16)
    ssm = _dot(glu, wso_ref[...])
    att = _dot(o_ref[...], wao_ref[...])
    mix = (ga_ref[...].astype(F32) * ssm + gb_ref[...].astype(F32) * att).astype(BF16)
    out_ref[...] = x_ref[...] + _dot(mix, wo_ref[...])


def _mix(x, ys, o, gates, w_glu, w_ssm_out, w_attn_out, w_o, *, tm):
    m, d = x.shape
    d_ssm = ys.shape[1]
    d_att = o.shape[1]

    def resident(w):
        return pl.BlockSpec(w.shape, lambda i: (0, 0), pipeline_mode=pl.Buffered(1))

    return pl.pallas_call(
        _mix_kernel,
        grid=(m // tm,),
        in_specs=[
            pl.BlockSpec((tm, d), lambda i: (i, 0)),
            pl.BlockSpec((tm, d_ssm), lambda i: (i, 0)),
            pl.BlockSpec((tm, d_att), lambda i: (i, 0)),
            pl.BlockSpec((tm, d), lambda i: (i, 0)),
            pl.BlockSpec((tm, d), lambda i: (i, 1)),
            resident(w_glu), resident(w_ssm_out), resident(w_attn_out), resident(w_o),
        ],
        out_specs=pl.BlockSpec((tm, d), lambda i: (i, 0)),
        out_shape=jax.ShapeDtypeStruct((m, d), F32),
        compiler_params=_params(("parallel",)),
        name="mix",
    )(x, ys, o, gates, gates, w_glu, w_ssm_out, w_attn_out, w_o)


def _rotary_tables(pos, rows):
    half = ROT_DIM // 2
    inv = ROPE_THETA ** (-jnp.arange(half, dtype=F32) / half)
    ang = pos.astype(F32)[:, None] * inv[None, :]
    cos, sin = jnp.cos(ang), jnp.sin(ang)
    t = pos.shape[0]
    zh = jnp.zeros((t, half), F32)
    rest0 = jnp.zeros((t, HEAD_DIM - ROT_DIM), F32)
    cos_c = jnp.concatenate([cos, cos, jnp.ones((t, HEAD_DIM - ROT_DIM), F32)], axis=1)
    s1_c = jnp.concatenate([-sin, zh, rest0], axis=1)
    s2_c = jnp.concatenate([zh, sin, rest0], axis=1)
    return tuple(jnp.broadcast_to(jnp.tile(a, (1, 2)), (rows, 2 * HEAD_DIM)) for a in (cos_c, s1_c, s2_c))


def kernel(x_prompt, x_sample, cache_k, cache_v, state_ssm_re, state_ssm_im, page_table, norm_ffn1, ffn1_w1, ffn1_w3, ffn1_w2, norm_mix, w_in, lam_q1, lam_k1, lam_q2, lam_k2, g_subln, w_attn_out, ssm_a_re, ssm_a_im, ssm_log_dt, ssm_b_re, ssm_b_im, ssm_c_re, ssm_c_im, ssm_d, w_glu, w_ssm_out, w_o, norm_ffn2, ffn2_w1, ffn2_w3, ffn2_w2, norm_final):
    batch, seq, d = x_prompt.shape
    n_dec, dec_seq, _ = x_sample.shape
    assert dec_seq == 1
    depth = cache_k.shape[0]
    n_heads, d_k = cache_k.shape[3], cache_k.shape[4]
    d_v = cache_v.shape[4]
    n_groups, n_state = ssm_a_re.shape[1], ssm_a_re.shape[2]
    d_qk, d_att, d_ssm = n_heads * d_k, n_heads * d_v, n_groups * SSM_GROUP
    past_len = page_table.shape[1] * cache_k.shape[2]
    m_p = batch * seq

    tabs_p = _rotary_tables(jnp.arange(seq, dtype=jnp.int32), seq)
    tabs_s = _rotary_tables(past_len + jnp.arange(dec_seq, dtype=jnp.int32), n_dec)
    row = lambda a: a.reshape(1, -1)
    gfin = row(norm_final)

    h_p = x_prompt.reshape(m_p, d)
    h_s = x_sample.reshape(n_dec, d)
    outs = [[] for _ in range(8)]
    for l in range(depth):
        lam_init = 0.8 - 0.6 * math.exp(-0.3 * l)
        last = l == depth - 1
        w11, w13, w12 = ffn1_w1[l].astype(BF16), ffn1_w3[l].astype(BF16), ffn1_w2[l].astype(BF16)
        w21, w23, w22 = ffn2_w1[l].astype(BF16), ffn2_w3[l].astype(BF16), ffn2_w2[l].astype(BF16)
        win = w_in[l].astype(BF16)
        wglu, wso = w_glu[l].astype(BF16), w_ssm_out[l].astype(BF16)
        wao, wo = w_attn_out[l].astype(BF16), w_o[l].astype(BF16)
        lam4 = jnp.stack([lam_q1[l], lam_k1[l], lam_q2[l], lam_k2[l]])
        gs = row(g_subln[l])
        lay = _s5_layout(ssm_a_re[l], ssm_a_im[l], ssm_log_dt[l], ssm_b_re[l], ssm_b_im[l],
                         ssm_c_re[l], ssm_c_im[l], ssm_d[l])

        def half1(x, tm):
            return _ffn(x, row(norm_ffn1[l]), w11, w13, w12, gfin, final_norm=False, tm=tm, tf=512)

        def half2(x, tm):
            return _ffn(x, row(norm_ffn2[l]), w21, w23, w22, gfin, final_norm=last, tm=tm, tf=512)

        x1 = half1(h_p, 512)
        q, k, kb, v, vb, u, gates = _project(x1, row(norm_mix[l]), win, *tabs_p, tm=512,
                                             d_qk=d_qk, d_v=d_att, d_ssm=d_ssm)
        o = _attn_prompt(q, kb, vb, lam4, gs, batch=batch, seq=seq, tq=512, lam_init=lam_init)
        ys, hre, him = _s5_prompt(u, lay, batch=batch, seq=seq, rows=1024)
        x2 = _mix(x1, ys, o, gates, wglu, wso, wao, wo, tm=256)
        h_p = half2(x2, 512)
        outs[0].append(k.reshape(batch, seq, n_heads, d_k))
        outs[1].append(v.reshape(batch, seq, n_heads, d_v))
        outs[2].append(hre.reshape(batch, n_groups, n_state))
        outs[3].append(him.reshape(batch, n_groups, n_state))

        x1 = half1(h_s, n_dec)
        q, k, kb, v, vb, u, gates = _project(x1, row(norm_mix[l]), win, *tabs_s, tm=n_dec,
                                             d_qk=d_qk, d_v=d_att, d_ssm=d_ssm)
        o = _attn_decode(q.astype(F32).reshape(n_dec, n_heads, d_k), k.reshape(n_dec, n_heads, d_k),
                         v.reshape(n_dec, n_heads, d_v), cache_k, cache_v, page_table, lam4, gs,
                         layer=l, lam_init=lam_init)
        ys, hre, him = _s5_step(u, state_ssm_re[l].reshape(n_dec, n_groups * n_state),
                                state_ssm_im[l].reshape(n_dec, n_groups * n_state), lay)
        x2 = _mix(x1, ys, o.reshape(n_dec, d_att).astype(BF16), gates, wglu, wso, wao, wo, tm=n_dec)
        h_s = half2(x2, n_dec)
        outs[4].append(k.reshape(n_dec, dec_seq, n_heads, d_k))
        outs[5].append(v.reshape(n_dec, dec_seq, n_heads, d_v))
        outs[6].append(hre.reshape(n_dec, n_groups, n_state))
        outs[7].append(him.reshape(n_dec, n_groups, n_state))

    return (h_p.reshape(batch, seq, d), h_s.reshape(n_dec, dec_seq, d)) + tuple(jnp.stack(o) for o in outs)
```

```python
import functools
import math

import jax
import jax.numpy as jnp
from jax import lax
from jax.experimental import pallas as pl
from jax.experimental.pallas import tpu as pltpu

F32 = jnp.float32
BF16 = jnp.bfloat16

HEAD_DIM = 64
ROT_DIM = HEAD_DIM // 4
ROPE_THETA = 500000.0
EPS = 1e-6
SSM_GROUP = 16
SSM_GROUP_BLOCK = 8

V7X_LANES = 128
V7X_SUBLANES = 8
V7X_MXU_WIDTH = 256
MXU_WIDTH = V7X_MXU_WIDTH
V7X_VMEM_BYTES = 64 * 1024 * 1024
VMEM_LIMIT = V7X_VMEM_BYTES * 3 // 4

NEG_BIG = -1e30
Q_SCALE = HEAD_DIM ** -0.5 * math.log2(math.e)


def _dot(a, b):
    return jnp.dot(a, b, preferred_element_type=F32)


def _dot_nt(a, b):
    return lax.dot_general(a, b, (((1,), (1,)), ((), ())), preferred_element_type=F32)


def _rms(x, g):
    return x * lax.rsqrt(jnp.mean(x * x, axis=-1, keepdims=True) + EPS) * g


def _params(sem):
    return pltpu.CompilerParams(dimension_semantics=sem, vmem_limit_bytes=VMEM_LIMIT)


def _ffn_kernel(x_ref, g_ref, w1_ref, w3_ref, w2_ref, gf_ref, o_ref, xn_s, acc_s, *, final_norm):
    f = pl.program_id(1)

    @pl.when(f == 0)
    def _():
        xn_s[...] = _rms(x_ref[...], g_ref[...]).astype(BF16)
        acc_s[...] = jnp.zeros_like(acc_s)

    xn = xn_s[...]
    acc = acc_s[...]
    for c in range(w1_ref.shape[1] // MXU_WIDTH):
        cols = slice(c * MXU_WIDTH, (c + 1) * MXU_WIDTH)
        h1 = _dot(xn, w1_ref[:, cols])
        h3 = _dot(xn, w3_ref[:, cols])
        hh = (h1 * jax.nn.sigmoid(h1) * h3).astype(BF16)
        acc = acc + _dot(hh, w2_ref[cols, :])
    acc_s[...] = acc

    @pl.when(f == pl.num_programs(1) - 1)
    def _():
        y = x_ref[...] + 0.5 * acc_s[...]
        if final_norm:
            y = _rms(y, gf_ref[...])
        o_ref[...] = y


def _ffn(x, g, w1, w3, w2, gf, *, final_norm, tm, tf):
    m, d = x.shape
    dff = w1.shape[1]
    return pl.pallas_call(
        functools.partial(_ffn_kernel, final_norm=final_norm),
        grid=(m // tm, dff // tf),
        in_specs=[
            pl.BlockSpec((tm, d), lambda i, f: (i, 0)),
            pl.BlockSpec((1, d), lambda i, f: (0, 0)),
            pl.BlockSpec((d, tf), lambda i, f: (0, f)),
            pl.BlockSpec((d, tf), lambda i, f: (0, f)),
            pl.BlockSpec((tf, d), lambda i, f: (f, 0)),
            pl.BlockSpec((1, d), lambda i, f: (0, 0)),
        ],
        out_specs=pl.BlockSpec((tm, d), lambda i, f: (i, 0)),
        out_shape=jax.ShapeDtypeStruct((m, d), F32),
        scratch_shapes=[pltpu.VMEM((tm, d), BF16), pltpu.VMEM((tm, d), F32)],
        compiler_params=_params(("parallel", "arbitrary")),
        name="ffn",
    )(x, g, w1, w3, w2, gf)


def _rotate(a, cos_ref, s1_ref, s2_ref):
    cos, s1, s2 = cos_ref[...], s1_ref[...], s2_ref[...]
    half = ROT_DIM // 2
    out = []
    for h in range(a.shape[1] // V7X_LANES):
        blk = a[:, h * V7X_LANES:(h + 1) * V7X_LANES]
        up = pltpu.roll(blk, V7X_LANES - half, 1)
        dn = pltpu.roll(blk, half, 1)
        out.append(blk * cos + up * s1 + dn * s2)
    return jnp.concatenate(out, axis=1)


def _col_chunks(n):
    return [slice(c, c + MXU_WIDTH) for c in range(0, n, MXU_WIDTH)]


def _proj_qk_kernel(x_ref, g_ref, w_ref, cos_ref, s1_ref, s2_ref, q_ref, k_ref, kb_ref, xn_s):
    xn_s[...] = _rms(x_ref[...], g_ref[...]).astype(BF16)
    n = q_ref.shape[1]
    for cols in _col_chunks(n):
        r = _rotate(_dot(xn_s[...], w_ref[:, cols]), cos_ref, s1_ref, s2_ref)
        q_ref[:, cols] = (r * Q_SCALE).astype(BF16)
    for cols in _col_chunks(n):
        r = _rotate(_dot(xn_s[...], w_ref[:, slice(n + cols.start, n + cols.stop)]), cos_ref, s1_ref, s2_ref)
        k_ref[:, cols] = r
        kb_ref[:, cols] = r.astype(BF16)


def _proj_vu_kernel(x_ref, g_ref, w_ref, v_ref, vb_ref, u_ref, xn_s):
    xn_s[...] = _rms(x_ref[...], g_ref[...]).astype(BF16)
    n = v_ref.shape[1]
    for cols in _col_chunks(n):
        a = _dot(xn_s[...], w_ref[:, cols])
        v_ref[:, cols] = a
        vb_ref[:, cols] = a.astype(BF16)
    for cols in _col_chunks(u_ref.shape[1]):
        u_ref[:, cols] = _dot(xn_s[...], w_ref[:, slice(n + cols.start, n + cols.stop)])


def _proj_gate_kernel(x_ref, g_ref, w_ref, s_ref, xn_s):
    @pl.when(pl.program_id(1) == 0)
    def _():
        xn_s[...] = _rms(x_ref[...], g_ref[...]).astype(BF16)

    for cols in _col_chunks(s_ref.shape[1]):
        s_ref[:, cols] = jax.nn.sigmoid(_dot(xn_s[...], w_ref[:, cols])).astype(BF16)


def _project(x, g, w_in, cos_t, s1_t, s2_t, *, tm, d_qk, d_v, d_ssm):
    m, d = x.shape
    tn = 2 * d_qk
    assert d_v + d_ssm == tn and (w_in.shape[1] - 2 * tn) % tn == 0
    n_gate = (w_in.shape[1] - 2 * tn) // tn
    n_t = cos_t.shape[0] // tm
    x_spec = pl.BlockSpec((tm, d), lambda i, j: (i, 0))
    g_spec = pl.BlockSpec((1, d), lambda i, j: (0, 0))
    tab_spec = pl.BlockSpec((tm, V7X_LANES), lambda i, j: (i % n_t, 0))
    xn_scratch = [pltpu.VMEM((tm, d), BF16)]

    def row_spec(n):
        return pl.BlockSpec((tm, n), lambda i, j: (i, 0))

    def slab_spec(first):
        return pl.BlockSpec((d, tn), lambda i, j: (0, first + j))

    q, k, kb = pl.pallas_call(
        _proj_qk_kernel,
        grid=(m // tm, 1),
        in_specs=[x_spec, g_spec, slab_spec(0), tab_spec, tab_spec, tab_spec],
        out_specs=[row_spec(d_qk)] * 3,
        out_shape=[jax.ShapeDtypeStruct((m, d_qk), BF16), jax.ShapeDtypeStruct((m, d_qk), F32),
                   jax.ShapeDtypeStruct((m, d_qk), BF16)],
        scratch_shapes=xn_scratch,
        compiler_params=_params(("parallel", "arbitrary")),
        name="proj_qk",
    )(x, g, w_in, cos_t, s1_t, s2_t)

    v, vb, u = pl.pallas_call(
        _proj_vu_kernel,
        grid=(m // tm, 1),
        in_specs=[x_spec, g_spec, slab_spec(1)],
        out_specs=[row_spec(d_v), row_spec(d_v), row_spec(d_ssm)],
        out_shape=[jax.ShapeDtypeStruct((m, d_v), F32), jax.ShapeDtypeStruct((m, d_v), BF16),
                   jax.ShapeDtypeStruct((m, d_ssm), F32)],
        scratch_shapes=xn_scratch,
        compiler_params=_params(("parallel", "arbitrary")),
        name="proj_vu",
    )(x, g, w_in)

    gates = pl.pallas_call(
        _proj_gate_kernel,
        grid=(m // tm, n_gate),
        in_specs=[x_spec, g_spec, slab_spec(2)],
        out_specs=pl.BlockSpec((tm, tn), lambda i, j: (i, j)),
        out_shape=jax.ShapeDtypeStruct((m, n_gate * tn), BF16),
        scratch_shapes=xn_scratch,
        compiler_params=_params(("parallel", "arbitrary")),
        name="proj_gate",
    )(x, g, w_in)
    return q, k, kb, v, vb, u, gates


def _lam(lam_ref, lam_init):
    a = jnp.sum(lam_ref[0:1, :] * lam_ref[1:2, :], axis=-1, keepdims=True)
    b = jnp.sum(lam_ref[2:3, :] * lam_ref[3:4, :], axis=-1, keepdims=True)
    return jnp.exp(a) - jnp.exp(b) + lam_init


def _subln(o, gs, lam_init):
    return _rms(o, gs) * (1.0 - lam_init)


def _split_components(q):
    lane = lax.broadcasted_iota(jnp.int32, q.shape, 1)
    zero = jnp.zeros_like(q)
    return jnp.concatenate([jnp.where(lane < HEAD_DIM, q, zero), jnp.where(lane >= HEAD_DIM, q, zero)], axis=0)


def _softmax_update(s, v_ext, m_s, acc_s):
    m_prev = m_s[...]
    m_new = jnp.maximum(m_prev, jnp.max(s, axis=-1, keepdims=True))
    alpha = jnp.exp2(m_prev - m_new)
    p = jnp.exp2((s - jnp.tile(m_new, (1, s.shape[1] // V7X_LANES))).astype(BF16))
    acc_s[...] = jnp.tile(alpha, (1, 2)) * acc_s[...] + _dot(p, v_ext)
    m_s[...] = m_new


def _attn_prompt_kernel(lam_ref, gs_ref, q_ref, k_ref, v_ref, o_ref, qq_s, ve_s, s_s, m_s, acc_s, *, tq, lam_init):
    i = pl.program_id(2)
    dv = v_ref.shape[1]

    @pl.when(i == 0)
    def _():
        ve_s[:, 0:dv] = v_ref[...]
        ve_s[:, dv:] = jnp.ones((ve_s.shape[0], ve_s.shape[1] - dv), BF16)

    qq_s[...] = _split_components(q_ref[...])
    m_s[...] = jnp.full_like(m_s, NEG_BIG)
    acc_s[...] = jnp.zeros_like(acc_s)

    def tile(ref, j):
        return ref[pl.ds(pl.multiple_of(j * tq, tq), tq), :]

    def scores(j, slot):
        s_s[slot] = _dot_nt(qq_s[...], tile(k_ref, j))

    def consume(j, slot, mask=None):
        s = s_s[slot]
        if mask is not None:
            s = jnp.where(mask, s, NEG_BIG)
        _softmax_update(s, tile(ve_s, j), m_s, acc_s)

    scores(0, 0)

    def pair(jj, c):
        j = 2 * jj
        scores(j + 1, 1)
        consume(j, 0)
        scores(j + 2, 0)
        consume(j + 1, 1)
        return c

    lax.fori_loop(0, i // 2, pair, 0)

    @pl.when(i % 2 == 1)
    def _():
        scores(i, 1)
        consume(i - 1, 0)
        s_s[0] = s_s[1]

    row = lax.broadcasted_iota(jnp.int32, s_s.shape[1:], 0)
    row = jnp.where(row >= tq, row - tq, row)
    col = lax.broadcasted_iota(jnp.int32, s_s.shape[1:], 1)
    consume(i, 0, mask=col <= row)

    o1 = acc_s[0:tq, 0:dv] / acc_s[0:tq, dv:]
    o2 = acc_s[tq:2 * tq, 0:dv] / acc_s[tq:2 * tq, dv:]
    o = o1 - _lam(lam_ref, lam_init) * o2
    o_ref[...] = _subln(o, gs_ref[...], lam_init).astype(BF16)


def _attn_prompt(qb, kb, vb, lam4, gs, *, batch, seq, tq, lam_init):
    m, dq = qb.shape
    n_heads = dq // V7X_LANES
    nq = seq // tq
    return pl.pallas_call(
        functools.partial(_attn_prompt_kernel, tq=tq, lam_init=lam_init),
        grid=(batch, n_heads, nq),
        in_specs=[
            pl.BlockSpec(lam4.shape, lambda b, h, i: (0, 0)),
            pl.BlockSpec((1, V7X_LANES), lambda b, h, i: (0, 0)),
            pl.BlockSpec((tq, V7X_LANES), lambda b, h, i: (b * nq + i, h)),
            pl.BlockSpec((seq, V7X_LANES), lambda b, h, i: (b, h)),
            pl.BlockSpec((seq, V7X_LANES), lambda b, h, i: (b, h)),
        ],
        out_specs=pl.BlockSpec((tq, V7X_LANES), lambda b, h, i: (b * nq + i, h)),
        out_shape=jax.ShapeDtypeStruct((m, dq), BF16),
        scratch_shapes=[pltpu.VMEM((2 * tq, V7X_LANES), BF16), pltpu.VMEM((seq, 2 * V7X_LANES), BF16),
                        pltpu.VMEM((2, 2 * tq, tq), F32), pltpu.VMEM((2 * tq, V7X_LANES), F32),
                        pltpu.VMEM((2 * tq, 2 * V7X_LANES), F32)],
        compiler_params=_params(("arbitrary", "arbitrary", "arbitrary")),
        name="attn_prompt",
    )(lam4, gs, qb, kb, vb)


def _attn_decode_kernel(pt_ref, lam_ref, gs_ref, q_ref, kn_ref, vn_ref, *refs, n_pages, lam_init):
    del pt_ref
    kc_refs, vc_refs, o_ref = refs[:n_pages], refs[n_pages:2 * n_pages], refs[2 * n_pages]
    n_heads = q_ref.shape[0]
    page, _, dk = kc_refs[0].shape
    dv = vc_refs[0].shape[2]

    qm = _split_components(q_ref[...])
    qmb = qm.astype(BF16)
    shape = (2 * n_heads, page * n_heads)
    row = lax.broadcasted_iota(jnp.int32, shape, 0)
    col = lax.broadcasted_iota(jnp.int32, shape, 1)
    same_head = ((row ^ col) & (n_heads - 1)) == 0

    kn2 = jnp.concatenate([kn_ref[...], kn_ref[...]], axis=0)
    vn2 = jnp.concatenate([vn_ref[...], vn_ref[...]], axis=0)
    s_self = jnp.sum(qm * kn2, axis=-1, keepdims=True)

    scores = []
    m = s_self
    for kc_ref in kc_refs:
        k2 = kc_ref[...].reshape(page * n_heads, dk).astype(BF16)
        s = jnp.where(same_head, _dot_nt(qmb, k2), NEG_BIG)
        scores.append(s)
        m = jnp.maximum(m, jnp.max(s, axis=-1, keepdims=True))

    p_self = jnp.exp2(s_self - m)
    l = p_self
    acc = p_self * vn2
    for s, vc_ref in zip(scores, vc_refs):
        p = jnp.exp2(s - m)
        l = l + jnp.sum(p, axis=-1, keepdims=True)
        acc = acc + _dot(p.astype(BF16), vc_ref[...].reshape(page * n_heads, dv).astype(BF16))

    o = acc / l
    o = o[0:n_heads, :] - _lam(lam_ref, lam_init) * o[n_heads:2 * n_heads, :]
    o_ref[...] = _subln(o, gs_ref[...], lam_init)


def _attn_decode(q3, kn3, vn3, cache_k, cache_v, page_table, lam4, gs, *, layer, lam_init):
    n_b, n_heads, dk = q3.shape
    assert n_heads & (n_heads - 1) == 0
    n_pages = page_table.shape[1]
    page = cache_k.shape[2]
    dv = cache_v.shape[4]
    row_spec = pl.BlockSpec((None, n_heads, dk), lambda b, pt: (b, 0, 0))

    def page_specs(d):
        return [pl.BlockSpec((None, None, page, n_heads, d), lambda b, pt, p=p: (layer, pt[b, p], 0, 0, 0))
                for p in range(n_pages)]

    grid_spec = pltpu.PrefetchScalarGridSpec(
        num_scalar_prefetch=1,
        grid=(n_b,),
        in_specs=[
            pl.BlockSpec(lam4.shape, lambda b, pt: (0, 0)),
            pl.BlockSpec((1, dv), lambda b, pt: (0, 0)),
            row_spec, row_spec,
            pl.BlockSpec((None, n_heads, dv), lambda b, pt: (b, 0, 0)),
            *page_specs(dk), *page_specs(dv),
        ],
        out_specs=pl.BlockSpec((None, n_heads, dv), lambda b, pt: (b, 0, 0)),
    )
    return pl.pallas_call(
        functools.partial(_attn_decode_kernel, n_pages=n_pages, lam_init=lam_init),
        grid_spec=grid_spec,
        out_shape=jax.ShapeDtypeStruct((n_b, n_heads, dv), F32),
        compiler_params=_params(("parallel",)),
        name="attn_decode",
    )(page_table, lam4, gs, q3, kn3, vn3, *([cache_k] * n_pages), *([cache_v] * n_pages))


def _s5_discretize(are, aim, ldt):
    dt = jnp.exp(ldt)
    mag = jnp.exp(dt * are)
    lbr = mag * jnp.cos(dt * aim)
    lbi = mag * jnp.sin(dt * aim)
    nr = lbr - 1.0
    den = are * are + aim * aim
    cr = (nr * are + lbi * aim) / den
    ci = (lbi * are - nr * aim) / den
    return lbr, lbi, cr, ci


def _s5_input_weight(cr, ci, bre, bim):
    return jnp.concatenate([cr * bre - ci * bim, cr * bim + ci * bre], axis=1).astype(BF16)


def _s5_readout(hr, hi, cre_ref, cim_ref, d_ref, u):
    return (_dot_nt(hr.astype(BF16), cre_ref[...].astype(BF16))
            - _dot_nt(hi.astype(BF16), cim_ref[...].astype(BF16)) + d_ref[...] * u)


def _s5_prompt_kernel(are_ref, aim_ref, ldt_ref, bre_ref, bim_ref, cre_ref, cim_ref, d_ref, u_ref,
                      y_ref, hre_ref, him_ref, up_s, bur_s, bui_s, yp_s, wb_s, lam_s, car_s, *, seg):
    ch = pl.program_id(2)
    n_st = bur_s.shape[1]
    sub = V7X_SUBLANES

    @pl.when(ch == 0)
    def _():
        lbr, lbi, cr, ci = _s5_discretize(are_ref[...], aim_ref[...], ldt_ref[...])
        wb_s[...] = _s5_input_weight(cr, ci, bre_ref[...], bim_ref[...])
        lam_s[0:1, :] = lbr
        lam_s[1:2, :] = lbi
        pr, pi = lbr, lbi
        for _ in range(int(math.log2(seg))):
            pr, pi = pr * pr - pi * pi, 2.0 * pr * pi
        lam_s[2:3, :] = pr
        lam_s[3:4, :] = pi
        car_s[...] = jnp.zeros_like(car_s)

    def permute(t, c):
        up_s[pl.ds(pl.multiple_of(t * sub, sub), sub), :] = u_ref[pl.ds(t, sub, stride=seg), :]
        return c

    lax.fori_loop(0, seg, permute, 0, unroll=8)

    bu = _dot(up_s[...].astype(BF16), wb_s[...])
    bur_s[...] = bu[:, :n_st]
    bui_s[...] = bu[:, n_st:]

    lbr = jnp.broadcast_to(lam_s[0:1, :], (sub, n_st))
    lbi = jnp.broadcast_to(lam_s[1:2, :], (sub, n_st))

    def advance(t, c):
        sr, si = c
        r0 = pl.multiple_of(t * sub, sub)
        return (lbr * sr - lbi * si + bur_s[pl.ds(r0, sub), :],
                lbr * si + lbi * sr + bui_s[pl.ds(r0, sub), :])

    zero = jnp.zeros((sub, n_st), F32)
    er, ei = lax.fori_loop(0, seg, advance, (zero, zero), unroll=4)

    psr, psi = lam_s[2:3, :], lam_s[3:4, :]
    hr, hi = car_s[0:1, :], car_s[1:2, :]
    rows_r, rows_i = [], []
    for r in range(sub):
        rows_r.append(hr)
        rows_i.append(hi)
        hr, hi = (er[r:r + 1, :] + psr * hr - psi * hi, ei[r:r + 1, :] + psr * hi + psi * hr)
    car_s[0:1, :] = hr
    car_s[1:2, :] = hi

    def advance_store(t, c):
        sr, si = advance(t, c)
        r0 = pl.multiple_of(t * sub, sub)
        bur_s[pl.ds(r0, sub), :] = sr
        bui_s[pl.ds(r0, sub), :] = si
        return sr, si

    lax.fori_loop(0, seg, advance_store,
                  (jnp.concatenate(rows_r, axis=0), jnp.concatenate(rows_i, axis=0)), unroll=4)

    yp_s[...] = _s5_readout(bur_s[...], bui_s[...], cre_ref, cim_ref, d_ref, up_s[...])

    def unpermute(t, c):
        y_ref[pl.ds(t, sub, stride=seg), :] = yp_s[pl.ds(pl.multiple_of(t * sub, sub), sub), :]
        return c

    lax.fori_loop(0, seg, unpermute, 0, unroll=8)

    @pl.when(ch == pl.num_programs(2) - 1)
    def _():
        hre_ref[...] = hr
        him_ref[...] = hi


def _s5_prompt(u, lay, *, batch, seq, rows):
    m, d_ssm = u.shape
    are, aim, ldt, bre, bim, cre, cim, dsk = lay
    n_gb, n_ch_lanes, n_st = bre.shape
    n_ch = seq // rows
    seg = rows // V7X_SUBLANES
    lane_spec = pl.BlockSpec((None, 1, n_st), lambda b, g, c: (g, 0, 0))
    mat_spec = pl.BlockSpec((None, n_ch_lanes, n_st), lambda b, g, c: (g, 0, 0))
    row_spec = pl.BlockSpec((rows, n_ch_lanes), lambda b, g, c: (b * n_ch + c, g))
    st_spec = pl.BlockSpec((None, None, 1, n_st), lambda b, g, c: (b, g, 0, 0))
    st_shape = jax.ShapeDtypeStruct((batch, n_gb, 1, n_st), F32)
    return pl.pallas_call(
        functools.partial(_s5_prompt_kernel, seg=seg),
        grid=(batch, n_gb, n_ch),
        in_specs=[lane_spec, lane_spec, lane_spec, mat_spec, mat_spec, mat_spec, mat_spec,
                  pl.BlockSpec((1, n_ch_lanes), lambda b, g, c: (0, g)), row_spec],
        out_specs=[row_spec, st_spec, st_spec],
        out_shape=[jax.ShapeDtypeStruct((m, d_ssm), F32), st_shape, st_shape],
        scratch_shapes=[pltpu.VMEM((rows, n_ch_lanes), F32), pltpu.VMEM((rows, n_st), F32),
                        pltpu.VMEM((rows, n_st), F32), pltpu.VMEM((rows, n_ch_lanes), F32),
                        pltpu.VMEM((n_ch_lanes, 2 * n_st), BF16), pltpu.VMEM((4, n_st), F32),
                        pltpu.VMEM((2, n_st), F32)],
        compiler_params=_params(("parallel", "parallel", "arbitrary")),
        name="s5_prompt",
    )(are, aim, ldt, bre, bim, cre, cim, dsk, u)


def _s5_step_kernel(are_ref, aim_ref, ldt_ref, bre_ref, bim_ref, cre_ref, cim_ref, d_ref, u_ref,
                    h0r_ref, h0i_ref, y_ref, hre_ref, him_ref):
    n_st = h0r_ref.shape[1]
    lbr, lbi, cr, ci = _s5_discretize(are_ref[...], aim_ref[...], ldt_ref[...])
    u = u_ref[...]
    bu = _dot(u.astype(BF16), _s5_input_weight(cr, ci, bre_ref[...], bim_ref[...]))
    h0r, h0i = h0r_ref[...], h0i_ref[...]
    hr = lbr * h0r - lbi * h0i + bu[:, :n_st]
    hi = lbr * h0i + lbi * h0r + bu[:, n_st:]
    hre_ref[...] = hr
    him_ref[...] = hi
    y_ref[...] = _s5_readout(hr, hi, cre_ref, cim_ref, d_ref, u)


def _s5_step(u, h0r, h0i, lay):
    m, d_ssm = u.shape
    are, aim, ldt, bre, bim, cre, cim, dsk = lay
    n_gb, n_ch_lanes, n_st = bre.shape
    lane_spec = pl.BlockSpec((None, 1, n_st), lambda g: (g, 0, 0))
    mat_spec = pl.BlockSpec((None, n_ch_lanes, n_st), lambda g: (g, 0, 0))
    row_spec = pl.BlockSpec((m, n_ch_lanes), lambda g: (0, g))
    st_spec = pl.BlockSpec((m, n_st), lambda g: (0, g))
    st_shape = jax.ShapeDtypeStruct(h0r.shape, F32)
    return pl.pallas_call(
        _s5_step_kernel,
        grid=(n_gb,),
        in_specs=[lane_spec, lane_spec, lane_spec, mat_spec, mat_spec, mat_spec, mat_spec,
                  pl.BlockSpec((1, n_ch_lanes), lambda g: (0, g)), row_spec, st_spec, st_spec],
        out_specs=[row_spec, st_spec, st_spec],
        out_shape=[jax.ShapeDtypeStruct((m, d_ssm), F32), st_shape, st_shape],
        compiler_params=_params(("parallel",)),
        name="s5_step",
    )(are, aim, ldt, bre, bim, cre, cim, dsk, u, h0r, h0i)


def _s5_layout(a_re, a_im, log_dt, b_re, b_im, c_re, c_im, d_skip):
    n_g, n_st = a_re.shape
    gl = SSM_GROUP_BLOCK
    n_gb = n_g // gl
    same = jnp.eye(gl, dtype=jnp.bool_)[None, :, None, :, None]

    def lanes(a):
        return a.reshape(n_gb, 1, gl * n_st)

    def block_diag(w):
        w5 = w.reshape(n_gb, gl, SSM_GROUP, 1, n_st)
        return jnp.where(same, w5, 0.0).reshape(n_gb, gl * SSM_GROUP, gl * n_st)

    return (lanes(a_re), lanes(a_im), lanes(jnp.broadcast_to(log_dt[:, None], (n_g, n_st))),
            block_diag(b_re.transpose(0, 2, 1)), block_diag(b_im.transpose(0, 2, 1)),
            block_diag(c_re), block_diag(c_im), d_skip.reshape(1, n_g * SSM_GROUP))


def _mix_kernel(x_ref, ys_ref, o_ref, ga_ref, gb_ref, wglu_ref, wso_ref, wao_ref, wo_ref, out_ref):
    ys = ys_ref[...]
    ya = 0.5 * ys * (1.0 + lax.erf(ys * (2.0 ** -0.5)))
    glu = (ya * jax.nn.sigmoid(_dot(ya.astype(BF16), wglu_ref[...]))).astype(BF16)
    ssm = _dot(glu, wso_ref[...])
    att = _dot(o_ref[...], wao_ref[...])
    mix = (ga_ref[...].astype(F32) * ssm + gb_ref[...].astype(F32) * att).astype(BF16)
    out_ref[...] = x_ref[...] + _dot(mix, wo_ref[...])


def _mix(x, ys, o, gates, w_glu, w_ssm_out, w_attn_out, w_o, *, tm):
    m, d = x.shape
    d_ssm = ys.shape[1]
    d_att = o.shape[1]

    def resident(w):
        return pl.BlockSpec(w.shape, lambda i: (0, 0), pipeline_mode=pl.Buffered(1))

    return pl.pallas_call(
        _mix_kernel,
        grid=(m // tm,),
        in_specs=[
            pl.BlockSpec((tm, d), lambda i: (i, 0)),
            pl.BlockSpec((tm, d_ssm), lambda i: (i, 0)),
            pl.BlockSpec((tm, d_att), lambda i: (i, 0)),
            pl.BlockSpec((tm, d), lambda i: (i, 0)),
            pl.BlockSpec((tm, d), lambda i: (i, 1)),
            resident(w_glu), resident(w_ssm_out), resident(w_attn_out), resident(w_o),
        ],
        out_specs=pl.BlockSpec((tm, d), lambda i: (i, 0)),
        out_shape=jax.ShapeDtypeStruct((m, d), F32),
        compiler_params=_params(("parallel",)),
        name="mix",
    )(x, ys, o, gates, gates, w_glu, w_ssm_out, w_attn_out, w_o)


def _rotary_tables(pos, rows):
    half = ROT_DIM // 2
    inv = ROPE_THETA ** (-jnp.arange(half, dtype=F32) / half)
    ang = pos.astype(F32)[:, None] * inv[None, :]
    cos, sin = jnp.cos(ang), jnp.sin(ang)
    t = pos.shape[0]
    zh = jnp.zeros((t, half), F32)
    rest0 = jnp.zeros((t, HEAD_DIM - ROT_DIM), F32)
    cos_c = jnp.concatenate([cos, cos, jnp.ones((t, HEAD_DIM - ROT_DIM), F32)], axis=1)
    s1_c = jnp.concatenate([-sin, zh, rest0], axis=1)
    s2_c = jnp.concatenate([zh, sin, rest0], axis=1)
    return tuple(jnp.broadcast_to(jnp.tile(a, (1, 2)), (rows, 2 * HEAD_DIM)) for a in (cos_c, s1_c, s2_c))


def kernel(x_prompt, x_sample, cache_k, cache_v, state_ssm_re, state_ssm_im, page_table, norm_ffn1, ffn1_w1, ffn1_w3, ffn1_w2, norm_mix, w_in, lam_q1, lam_k1, lam_q2, lam_k2, g_subln, w_attn_out, ssm_a_re, ssm_a_im, ssm_log_dt, ssm_b_re, ssm_b_im, ssm_c_re, ssm_c_im, ssm_d, w_glu, w_ssm_out, w_o, norm_ffn2, ffn2_w1, ffn2_w3, ffn2_w2, norm_final):
    batch, seq, d = x_prompt.shape
    n_dec, dec_seq, _ = x_sample.shape
    assert dec_seq == 1
    depth = cache_k.shape[0]
    n_heads, d_k = cache_k.shape[3], cache_k.shape[4]
    d_v = cache_v.shape[4]
    n_groups, n_state = ssm_a_re.shape[1], ssm_a_re.shape[2]
    d_qk, d_att, d_ssm = n_heads * d_k, n_heads * d_v, n_groups * SSM_GROUP
    past_len = page_table.shape[1] * cache_k.shape[2]
    m_p = batch * seq

    tabs_p = _rotary_tables(jnp.arange(seq, dtype=jnp.int32), seq)
    tabs_s = _rotary_tables(past_len + jnp.arange(dec_seq, dtype=jnp.int32), n_dec)
    row = lambda a: a.reshape(1, -1)
    gfin = row(norm_final)

    h_p = x_prompt.reshape(m_p, d)
    h_s = x_sample.reshape(n_dec, d)
    outs = [[] for _ in range(8)]
    for l in range(depth):
        lam_init = 0.8 - 0.6 * math.exp(-0.3 * l)
        last = l == depth - 1
        w11, w13, w12 = ffn1_w1[l].astype(BF16), ffn1_w3[l].astype(BF16), ffn1_w2[l].astype(BF16)
        w21, w23, w22 = ffn2_w1[l].astype(BF16), ffn2_w3[l].astype(BF16), ffn2_w2[l].astype(BF16)
        win = w_in[l].astype(BF16)
        wglu, wso = w_glu[l].astype(BF16), w_ssm_out[l].astype(BF16)
        wao, wo = w_attn_out[l].astype(BF16), w_o[l].astype(BF16)
        lam4 = jnp.stack([lam_q1[l], lam_k1[l], lam_q2[l], lam_k2[l]])
        gs = row(g_subln[l])
        lay = _s5_layout(ssm_a_re[l], ssm_a_im[l], ssm_log_dt[l], ssm_b_re[l], ssm_b_im[l],
                         ssm_c_re[l], ssm_c_im[l], ssm_d[l])

        def half1(x, tm):
            return _ffn(x, row(norm_ffn1[l]), w11, w13, w12, gfin, final_norm=False, tm=tm, tf=512)

        def half2(x, tm):
            return _ffn(x, row(norm_ffn2[l]), w21, w23, w22, gfin, final_norm=last, tm=tm, tf=512)

        x1 = half1(h_p, 512)
        q, k, kb, v, vb, u, gates = _project(x1, row(norm_mix[l]), win, *tabs_p, tm=512,
                                             d_qk=d_qk, d_v=d_att, d_ssm=d_ssm)
        o = _attn_prompt(q, kb, vb, lam4, gs, batch=batch, seq=seq, tq=512, lam_init=lam_init)
        ys, hre, him = _s5_prompt(u, lay, batch=batch, seq=seq, rows=1024)
        x2 = _mix(x1, ys, o, gates, wglu, wso, wao, wo, tm=256)
        h_p = half2(x2, 512)
        outs[0].append(k.reshape(batch, seq, n_heads, d_k))
        outs[1].append(v.reshape(batch, seq, n_heads, d_v))
        outs[2].append(hre.reshape(batch, n_groups, n_state))
        outs[3].append(him.reshape(batch, n_groups, n_state))

        x1 = half1(h_s, n_dec)
        q, k, kb, v, vb, u, gates = _project(x1, row(norm_mix[l]), win, *tabs_s, tm=n_dec,
                                             d_qk=d_qk, d_v=d_att, d_ssm=d_ssm)
        o = _attn_decode(q.astype(F32).reshape(n_dec, n_heads, d_k), k.reshape(n_dec, n_heads, d_k),
                         v.reshape(n_dec, n_heads, d_v), cache_k, cache_v, page_table, lam4, gs,
                         layer=l, lam_init=lam_init)
        ys, hre, him = _s5_step(u, state_ssm_re[l].reshape(n_dec, n_groups * n_state),
                                state_ssm_im[l].reshape(n_dec, n_groups * n_state), lay)
        x2 = _mix(x1, ys, o.reshape(n_dec, d_att).astype(BF16), gates, wglu, wso, wao, wo, tm=n_dec)
        h_s = half2(x2, n_dec)
        outs[4].append(k.reshape(n_dec, dec_seq, n_heads, d_k))
        outs[5].append(v.reshape(n_dec, dec_seq, n_heads, d_v))
        outs[6].append(hre.reshape(n_dec, n_groups, n_state))
        outs[7].append(him.reshape(n_dec, n_groups, n_state))

    return (h_p.reshape(batch, seq, d), h_s.reshape(n_dec, dec_seq, d)) + tuple(jnp.stack(o) for o in outs)
```

```python
import functools
import math

import jax
import jax.numpy as jnp
from jax import lax
from jax.experimental import pallas as pl
from jax.experimental.pallas import tpu as pltpu

F32 = jnp.float32
BF16 = jnp.bfloat16

HEAD_DIM = 64
ROT_DIM = HEAD_DIM // 4
ROPE_THETA = 500000.0
EPS = 1e-6
SSM_GROUP = 16
SSM_GROUP_BLOCK = 8

V7X_LANES = 128
V7X_SUBLANES = 8
V7X_MXU_WIDTH = 256
MXU_WIDTH = V7X_MXU_WIDTH
V7X_VMEM_BYTES = 64 * 1024 * 1024
VMEM_LIMIT = V7X_VMEM_BYTES * 3 // 4
ATTN_VMEM_LIMIT = V7X_VMEM_BYTES * 7 // 8

NEG_BIG = -1e30
Q_SCALE = HEAD_DIM ** -0.5 * math.log2(math.e)


def _dot(a, b):
    return jnp.dot(a, b, preferred_element_type=F32)


def _dot_nt(a, b):
    return lax.dot_general(a, b, (((1,), (1,)), ((), ())), preferred_element_type=F32)


def _rms(x, g):
    return x * lax.rsqrt(jnp.mean(x * x, axis=-1, keepdims=True) + EPS) * g


def _params(sem):
    return pltpu.CompilerParams(dimension_semantics=sem, vmem_limit_bytes=VMEM_LIMIT)


def _ffn_kernel(x_ref, g_ref, w1_ref, w3_ref, w2_ref, gf_ref, o_ref, xn_s, acc_s, *, final_norm):
    f = pl.program_id(1)

    @pl.when(f == 0)
    def _():
        xn_s[...] = _rms(x_ref[...], g_ref[...]).astype(BF16)
        acc_s[...] = jnp.zeros_like(acc_s)

    xn = xn_s[...]
    acc = acc_s[...]
    for c in range(w1_ref.shape[1] // MXU_WIDTH):
        cols = slice(c * MXU_WIDTH, (c + 1) * MXU_WIDTH)
        h1 = _dot(xn, w1_ref[:, cols])
        h3 = _dot(xn, w3_ref[:, cols])
        hh = (h1 * jax.nn.sigmoid(h1) * h3).astype(BF16)
        acc = acc + _dot(hh, w2_ref[cols, :])
    acc_s[...] = acc

    @pl.when(f == pl.num_programs(1) - 1)
    def _():
        y = x_ref[...] + 0.5 * acc_s[...]
        if final_norm:
            y = _rms(y, gf_ref[...])
        o_ref[...] = y


def _ffn(x, g, w1, w3, w2, gf, *, final_norm, tm, tf):
    m, d = x.shape
    dff = w1.shape[1]
    return pl.pallas_call(
        functools.partial(_ffn_kernel, final_norm=final_norm),
        grid=(m // tm, dff // tf),
        in_specs=[
            pl.BlockSpec((tm, d), lambda i, f: (i, 0)),
            pl.BlockSpec((1, d), lambda i, f: (0, 0)),
            pl.BlockSpec((d, tf), lambda i, f: (0, f)),
            pl.BlockSpec((d, tf), lambda i, f: (0, f)),
            pl.BlockSpec((tf, d), lambda i, f: (f, 0)),
            pl.BlockSpec((1, d), lambda i, f: (0, 0)),
        ],
        out_specs=pl.BlockSpec((tm, d), lambda i, f: (i, 0)),
        out_shape=jax.ShapeDtypeStruct((m, d), F32),
        scratch_shapes=[pltpu.VMEM((tm, d), BF16), pltpu.VMEM((tm, d), F32)],
        compiler_params=_params(("parallel", "arbitrary")),
        name="ffn",
    )(x, g, w1, w3, w2, gf)


def _rotate(a, cos_ref, s1_ref, s2_ref):
    cos, s1, s2 = cos_ref[...], s1_ref[...], s2_ref[...]
    half = ROT_DIM // 2
    out = []
    for h in range(a.shape[1] // V7X_LANES):
        blk = a[:, h * V7X_LANES:(h + 1) * V7X_LANES]
        up = pltpu.roll(blk, V7X_LANES - half, 1)
        dn = pltpu.roll(blk, half, 1)
        out.append(blk * cos + up * s1 + dn * s2)
    return jnp.concatenate(out, axis=1)


def _col_chunks(n):
    return [slice(c, c + MXU_WIDTH) for c in range(0, n, MXU_WIDTH)]


def _proj_qk_kernel(x_ref, g_ref, w_ref, cos_ref, s1_ref, s2_ref, q_ref, k_ref, kb_ref, xn_s):
    xn_s[...] = _rms(x_ref[...], g_ref[...]).astype(BF16)
    n = q_ref.shape[1]
    for cols in _col_chunks(n):
        r = _rotate(_dot(xn_s[...], w_ref[:, cols]), cos_ref, s1_ref, s2_ref)
        q_ref[:, cols] = (r * Q_SCALE).astype(BF16)
    for cols in _col_chunks(n):
        r = _rotate(_dot(xn_s[...], w_ref[:, slice(n + cols.start, n + cols.stop)]), cos_ref, s1_ref, s2_ref)
        k_ref[:, cols] = r
        kb_ref[:, cols] = r.astype(BF16)


def _proj_vu_kernel(x_ref, g_ref, w_ref, v_ref, vb_ref, u_ref, xn_s):
    xn_s[...] = _rms(x_ref[...], g_ref[...]).astype(BF16)
    n = v_ref.shape[1]
    for cols in _col_chunks(n):
        a = _dot(xn_s[...], w_ref[:, cols])
        v_ref[:, cols] = a
        vb_ref[:, cols] = a.astype(BF16)
    for cols in _col_chunks(u_ref.shape[1]):
        u_ref[:, cols] = _dot(xn_s[...], w_ref[:, slice(n + cols.start, n + cols.stop)])


def _proj_gate_kernel(x_ref, g_ref, w_ref, s_ref, xn_s):
    @pl.when(pl.program_id(1) == 0)
    def _():
        xn_s[...] = _rms(x_ref[...], g_ref[...]).astype(BF16)

    for cols in _col_chunks(s_ref.shape[1]):
        s_ref[:, cols] = jax.nn.sigmoid(_dot(xn_s[...], w_ref[:, cols])).astype(BF16)


def _project(x, g, w_in, cos_t, s1_t, s2_t, *, tm, d_qk, d_v, d_ssm):
    m, d = x.shape
    tn = 2 * d_qk
    assert d_v + d_ssm == tn and (w_in.shape[1] - 2 * tn) % tn == 0
    n_gate = (w_in.shape[1] - 2 * tn) // tn
    n_t = cos_t.shape[0] // tm
    x_spec = pl.BlockSpec((tm, d), lambda i, j: (i, 0))
    g_spec = pl.BlockSpec((1, d), lambda i, j: (0, 0))
    tab_spec = pl.BlockSpec((tm, V7X_LANES), lambda i, j: (i % n_t, 0))
    xn_scratch = [pltpu.VMEM((tm, d), BF16)]

    def row_spec(n):
        return pl.BlockSpec((tm, n), lambda i, j: (i, 0))

    def slab_spec(first):
        return pl.BlockSpec((d, tn), lambda i, j: (0, first + j))

    q, k, kb = pl.pallas_call(
        _proj_qk_kernel,
        grid=(m // tm, 1),
        in_specs=[x_spec, g_spec, slab_spec(0), tab_spec, tab_spec, tab_spec],
        out_specs=[row_spec(d_qk)] * 3,
        out_shape=[jax.ShapeDtypeStruct((m, d_qk), BF16), jax.ShapeDtypeStruct((m, d_qk), F32),
                   jax.ShapeDtypeStruct((m, d_qk), BF16)],
        scratch_shapes=xn_scratch,
        compiler_params=_params(("parallel", "arbitrary")),
        name="proj_qk",
    )(x, g, w_in, cos_t, s1_t, s2_t)

    v, vb, u = pl.pallas_call(
        _proj_vu_kernel,
        grid=(m // tm, 1),
        in_specs=[x_spec, g_spec, slab_spec(1)],
        out_specs=[row_spec(d_v), row_spec(d_v), row_spec(d_ssm)],
        out_shape=[jax.ShapeDtypeStruct((m, d_v), F32), jax.ShapeDtypeStruct((m, d_v), BF16),
                   jax.ShapeDtypeStruct((m, d_ssm), F32)],
        scratch_shapes=xn_scratch,
        compiler_params=_params(("parallel", "arbitrary")),
        name="proj_vu",
    )(x, g, w_in)

    gates = pl.pallas_call(
        _proj_gate_kernel,
        grid=(m // tm, n_gate),
        in_specs=[x_spec, g_spec, slab_spec(2)],
        out_specs=pl.BlockSpec((tm, tn), lambda i, j: (i, j)),
        out_shape=jax.ShapeDtypeStruct((m, n_gate * tn), BF16),
        scratch_shapes=xn_scratch,
        compiler_params=_params(("parallel", "arbitrary")),
        name="proj_gate",
    )(x, g, w_in)
    return q, k, kb, v, vb, u, gates


def _lam(lam_ref, lam_init):
    a = jnp.sum(lam_ref[0:1, :] * lam_ref[1:2, :], axis=-1, keepdims=True)
    b = jnp.sum(lam_ref[2:3, :] * lam_ref[3:4, :], axis=-1, keepdims=True)
    return jnp.exp(a) - jnp.exp(b) + lam_init


def _subln(o, gs, lam_init):
    return _rms(o, gs) * (1.0 - lam_init)


def _split_components(q):
    lane = lax.broadcasted_iota(jnp.int32, q.shape, 1)
    zero = jnp.zeros_like(q)
    return jnp.concatenate([jnp.where(lane < HEAD_DIM, q, zero), jnp.where(lane >= HEAD_DIM, q, zero)], axis=0)


def _softmax_update(s, v_ext, m_s, acc_s):
    m_prev = m_s[...]
    m_new = jnp.maximum(m_prev, jnp.max(s, axis=-1, keepdims=True))
    alpha = jnp.exp2(m_prev - m_new)
    p = jnp.exp2((s - jnp.tile(m_new, (1, s.shape[1] // V7X_LANES))).astype(BF16))
    acc_s[...] = jnp.tile(alpha, (1, 2)) * acc_s[...] + _dot(p, v_ext)
    m_s[...] = m_new


def _prompt_attention(lam_ref, gs_ref, q_ref, k_ref, v_ref, o_ref, qq_s, ve_s, s_s, m_s, acc_s, *, tq, lam_init):
    i = pl.program_id(2)
    dv = v_ref.shape[1]

    @pl.when(i == 0)
    def _():
        ve_s[:, 0:dv] = v_ref[...]
        ve_s[:, dv:] = jnp.ones((ve_s.shape[0], ve_s.shape[1] - dv), BF16)

    qq_s[...] = _split_components(q_ref[...])
    m_s[...] = jnp.full_like(m_s, NEG_BIG)
    acc_s[...] = jnp.zeros_like(acc_s)

    def tile(ref, j):
        return ref[pl.ds(pl.multiple_of(j * tq, tq), tq), :]

    def scores(j, slot):
        s_s[slot] = _dot_nt(qq_s[...], tile(k_ref, j))

    def consume(j, slot, mask=None):
        s = s_s[slot]
        if mask is not None:
            s = jnp.where(mask, s, NEG_BIG)
        _softmax_update(s, tile(ve_s, j), m_s, acc_s)

    scores(0, 0)

    def pair(jj, c):
        j = 2 * jj
        scores(j + 1, 1)
        consume(j, 0)
        scores(j + 2, 0)
        consume(j + 1, 1)
        return c

    lax.fori_loop(0, i // 2, pair, 0)

    @pl.when(i % 2 == 1)
    def _():
        scores(i, 1)
        consume(i - 1, 0)
        s_s[0] = s_s[1]

    row = lax.broadcasted_iota(jnp.int32, s_s.shape[1:], 0)
    row = jnp.where(row >= tq, row - tq, row)
    col = lax.broadcasted_iota(jnp.int32, s_s.shape[1:], 1)
    consume(i, 0, mask=col <= row)

    o1 = acc_s[0:tq, 0:dv] / acc_s[0:tq, dv:]
    o2 = acc_s[tq:2 * tq, 0:dv] / acc_s[tq:2 * tq, dv:]
    o = o1 - _lam(lam_ref, lam_init) * o2
    o_ref[...] = _subln(o, gs_ref[...], lam_init).astype(BF16)


def _decode_attention(lam_ref, gs_ref, q_ref, kn_ref, vn_ref, kc_refs, vc_refs, o_ref, *, lam_init):
    n_heads = q_ref.shape[0]
    page, _, dk = kc_refs[0].shape
    dv = vc_refs[0].shape[2]

    qm = _split_components(q_ref[...])
    qmb = qm.astype(BF16)
    shape = (2 * n_heads, page * n_heads)
    row = lax.broadcasted_iota(jnp.int32, shape, 0)
    col = lax.broadcasted_iota(jnp.int32, shape, 1)
    same_head = ((row ^ col) & (n_heads - 1)) == 0

    kn2 = jnp.concatenate([kn_ref[...], kn_ref[...]], axis=0)
    vn2 = jnp.concatenate([vn_ref[...], vn_ref[...]], axis=0)
    s_self = jnp.sum(qm * kn2, axis=-1, keepdims=True)

    scores = []
    m = s_self
    for kc_ref in kc_refs:
        k2 = kc_ref[...].reshape(page * n_heads, dk).astype(BF16)
        s = jnp.where(same_head, _dot_nt(qmb, k2), NEG_BIG)
        scores.append(s)
        m = jnp.maximum(m, jnp.max(s, axis=-1, keepdims=True))

    p_self = jnp.exp2(s_self - m)
    l = p_self
    acc = p_self * vn2
    for s, vc_ref in zip(scores, vc_refs):
        p = jnp.exp2(s - m)
        l = l + jnp.sum(p, axis=-1, keepdims=True)
        acc = acc + _dot(p.astype(BF16), vc_ref[...].reshape(page * n_heads, dv).astype(BF16))

    o = acc / l
    o = o[0:n_heads, :] - _lam(lam_ref, lam_init) * o[n_heads:2 * n_heads, :]
    o_ref[...] = _subln(o, gs_ref[...], lam_init)


def _attn_kernel(pt_ref, lam_ref, gs_ref, q_ref, k_ref, v_ref, qd_ref, knd_ref, vnd_ref, *refs,
                 n_pages, tq, lam_init):
    del pt_ref
    kc_refs, vc_refs = refs[:n_pages], refs[n_pages:2 * n_pages]
    o_ref, od_ref, qq_s, ve_s, s_s, m_s, acc_s = refs[2 * n_pages:]
    _decode_attention(lam_ref, gs_ref, qd_ref, knd_ref, vnd_ref, kc_refs, vc_refs, od_ref, lam_init=lam_init)
    _prompt_attention(lam_ref, gs_ref, q_ref, k_ref, v_ref, o_ref, qq_s, ve_s, s_s, m_s, acc_s,
                      tq=tq, lam_init=lam_init)


def _attention(qb, kb, vb, q3, kn3, vn3, cache_k, cache_v, page_table, lam4, gs, *, batch, seq, tq, layer, lam_init):
    m, dq = qb.shape
    n_heads = dq // V7X_LANES
    nq = seq // tq
    n_dec, _, dk = q3.shape
    n_pages = page_table.shape[1]
    page = cache_k.shape[2]
    dv = cache_v.shape[4]
    assert n_heads & (n_heads - 1) == 0 and dv == V7X_LANES
    assert n_dec == batch * n_heads * nq, "one decode sample per prompt-attention grid step"

    def sample(b, h, i):
        return (b * n_heads + h) * nq + i

    def sample_spec(d):
        return pl.BlockSpec((None, n_heads, d), lambda b, h, i, pt: (sample(b, h, i), 0, 0))

    def page_specs(d):
        return [pl.BlockSpec((None, None, page, n_heads, d),
                             lambda b, h, i, pt, p=p: (layer, pt[sample(b, h, i), p], 0, 0, 0))
                for p in range(n_pages)]

    tile_spec = pl.BlockSpec((tq, V7X_LANES), lambda b, h, i, pt: (b * nq + i, h))
    head_spec = pl.BlockSpec((seq, V7X_LANES), lambda b, h, i, pt: (b, h))
    grid_spec = pltpu.PrefetchScalarGridSpec(
        num_scalar_prefetch=1,
        grid=(batch, n_heads, nq),
        in_specs=[
            pl.BlockSpec(lam4.shape, lambda b, h, i, pt: (0, 0)),
            pl.BlockSpec((1, dv), lambda b, h, i, pt: (0, 0)),
            tile_spec, head_spec, head_spec,
            sample_spec(dk), sample_spec(dk), sample_spec(dv),
            *page_specs(dk), *page_specs(dv),
        ],
        out_specs=[tile_spec, sample_spec(dv)],
        scratch_shapes=[pltpu.VMEM((2 * tq, V7X_LANES), BF16), pltpu.VMEM((seq, 2 * V7X_LANES), BF16),
                        pltpu.VMEM((2, 2 * tq, tq), F32), pltpu.VMEM((2 * tq, V7X_LANES), F32),
                        pltpu.VMEM((2 * tq, 2 * V7X_LANES), F32)],
    )
    return pl.pallas_call(
        functools.partial(_attn_kernel, n_pages=n_pages, tq=tq, lam_init=lam_init),
        grid_spec=grid_spec,
        out_shape=[jax.ShapeDtypeStruct((m, dq), BF16), jax.ShapeDtypeStruct((n_dec, n_heads, dv), F32)],
        compiler_params=pltpu.CompilerParams(dimension_semantics=("arbitrary",) * 3,
                                             vmem_limit_bytes=ATTN_VMEM_LIMIT),
        name="attention",
    )(page_table, lam4, gs, qb, kb, vb, q3, kn3, vn3, *([cache_k] * n_pages), *([cache_v] * n_pages))


def _s5_discretize(are, aim, ldt):
    dt = jnp.exp(ldt)
    mag = jnp.exp(dt * are)
    lbr = mag * jnp.cos(dt * aim)
    lbi = mag * jnp.sin(dt * aim)
    nr = lbr - 1.0
    den = are * are + aim * aim
    cr = (nr * are + lbi * aim) / den
    ci = (lbi * are - nr * aim) / den
    return lbr, lbi, cr, ci


def _s5_input_weight(cr, ci, bre, bim):
    return jnp.concatenate([cr * bre - ci * bim, cr * bim + ci * bre], axis=1).astype(BF16)


def _s5_readout(hr, hi, cre_ref, cim_ref, d_ref, u):
    return (_dot_nt(hr.astype(BF16), cre_ref[...].astype(BF16))
            - _dot_nt(hi.astype(BF16), cim_ref[...].astype(BF16)) + d_ref[...] * u)


def _s5_prompt_kernel(are_ref, aim_ref, ldt_ref, bre_ref, bim_ref, cre_ref, cim_ref, d_ref, u_ref,
                      y_ref, hre_ref, him_ref, up_s, bur_s, bui_s, yp_s, wb_s, lam_s, car_s, *, seg):
    ch = pl.program_id(2)
    n_st = bur_s.shape[1]
    sub = V7X_SUBLANES

    @pl.when(ch == 0)
    def _():
        lbr, lbi, cr, ci = _s5_discretize(are_ref[...], aim_ref[...], ldt_ref[...])
        wb_s[...] = _s5_input_weight(cr, ci, bre_ref[...], bim_ref[...])
        lam_s[0:1, :] = lbr
        lam_s[1:2, :] = lbi
        pr, pi = lbr, lbi
        for _ in range(int(math.log2(seg))):
            pr, pi = pr * pr - pi * pi, 2.0 * pr * pi
        lam_s[2:3, :] = pr
        lam_s[3:4, :] = pi
        car_s[...] = jnp.zeros_like(car_s)

    def permute(t, c):
        up_s[pl.ds(pl.multiple_of(t * sub, sub), sub), :] = u_ref[pl.ds(t, sub, stride=seg), :]
        return c

    lax.fori_loop(0, seg, permute, 0, unroll=8)

    bu = _dot(up_s[...].astype(BF16), wb_s[...])
    bur_s[...] = bu[:, :n_st]
    bui_s[...] = bu[:, n_st:]

    lbr = jnp.broadcast_to(lam_s[0:1, :], (sub, n_st))
    lbi = jnp.broadcast_to(lam_s[1:2, :], (sub, n_st))

    def advance(t, c):
        sr, si = c
        r0 = pl.multiple_of(t * sub, sub)
        return (lbr * sr - lbi * si + bur_s[pl.ds(r0, sub), :],
                lbr * si + lbi * sr + bui_s[pl.ds(r0, sub), :])

    zero = jnp.zeros((sub, n_st), F32)
    er, ei = lax.fori_loop(0, seg, advance, (zero, zero), unroll=4)

    psr, psi = lam_s[2:3, :], lam_s[3:4, :]
    hr, hi = car_s[0:1, :], car_s[1:2, :]
    rows_r, rows_i = [], []
    for r in range(sub):
        rows_r.append(hr)
        rows_i.append(hi)
        hr, hi = (er[r:r + 1, :] + psr * hr - psi * hi, ei[r:r + 1, :] + psr * hi + psi * hr)
    car_s[0:1, :] = hr
    car_s[1:2, :] = hi

    def advance_store(t, c):
        sr, si = advance(t, c)
        r0 = pl.multiple_of(t * sub, sub)
        bur_s[pl.ds(r0, sub), :] = sr
        bui_s[pl.ds(r0, sub), :] = si
        return sr, si

    lax.fori_loop(0, seg, advance_store,
                  (jnp.concatenate(rows_r, axis=0), jnp.concatenate(rows_i, axis=0)), unroll=4)

    yp_s[...] = _s5_readout(bur_s[...], bui_s[...], cre_ref, cim_ref, d_ref, up_s[...])

    def unpermute(t, c):
        y_ref[pl.ds(t, sub, stride=seg), :] = yp_s[pl.ds(pl.multiple_of(t * sub, sub), sub), :]
        return c

    lax.fori_loop(0, seg, unpermute, 0, unroll=8)

    @pl.when(ch == pl.num_programs(2) - 1)
    def _():
        hre_ref[...] = hr
        him_ref[...] = hi


def _s5_prompt(u, lay, *, batch, seq, rows):
    m, d_ssm = u.shape
    are, aim, ldt, bre, bim, cre, cim, dsk = lay
    n_gb, n_ch_lanes, n_st = bre.shape
    n_ch = seq // rows
    seg = rows // V7X_SUBLANES
    lane_spec = pl.BlockSpec((None, 1, n_st), lambda b, g, c: (g, 0, 0))
    mat_spec = pl.BlockSpec((None, n_ch_lanes, n_st), lambda b, g, c: (g, 0, 0))
    row_spec = pl.BlockSpec((rows, n_ch_lanes), lambda b, g, c: (b * n_ch + c, g))
    st_spec = pl.BlockSpec((None, None, 1, n_st), lambda b, g, c: (b, g, 0, 0))
    st_shape = jax.ShapeDtypeStruct((batch, n_gb, 1, n_st), F32)
    return pl.pallas_call(
        functools.partial(_s5_prompt_kernel, seg=seg),
        grid=(batch, n_gb, n_ch),
        in_specs=[lane_spec, lane_spec, lane_spec, mat_spec, mat_spec, mat_spec, mat_spec,
                  pl.BlockSpec((1, n_ch_lanes), lambda b, g, c: (0, g)), row_spec],
        out_specs=[row_spec, st_spec, st_spec],
        out_shape=[jax.ShapeDtypeStruct((m, d_ssm), F32), st_shape, st_shape],
        scratch_shapes=[pltpu.VMEM((rows, n_ch_lanes), F32), pltpu.VMEM((rows, n_st), F32),
                        pltpu.VMEM((rows, n_st), F32), pltpu.VMEM((rows, n_ch_lanes), F32),
                        pltpu.VMEM((n_ch_lanes, 2 * n_st), BF16), pltpu.VMEM((4, n_st), F32),
                        pltpu.VMEM((2, n_st), F32)],
        compiler_params=_params(("parallel", "parallel", "arbitrary")),
        name="s5_prompt",
    )(are, aim, ldt, bre, bim, cre, cim, dsk, u)


def _s5_step_kernel(are_ref, aim_ref, ldt_ref, bre_ref, bim_ref, cre_ref, cim_ref, d_ref, u_ref,
                    h0r_ref, h0i_ref, y_ref, hre_ref, him_ref):
    n_st = h0r_ref.shape[1]
    lbr, lbi, cr, ci = _s5_discretize(are_ref[...], aim_ref[...], ldt_ref[...])
    u = u_ref[...]
    bu = _dot(u.astype(BF16), _s5_input_weight(cr, ci, bre_ref[...], bim_ref[...]))
    h0r, h0i = h0r_ref[...], h0i_ref[...]
    hr = lbr * h0r - lbi * h0i + bu[:, :n_st]
    hi = lbr * h0i + lbi * h0r + bu[:, n_st:]
    hre_ref[...] = hr
    him_ref[...] = hi
    y_ref[...] = _s5_readout(hr, hi, cre_ref, cim_ref, d_ref, u)


def _s5_step(u, h0r, h0i, lay):
    m, d_ssm = u.shape
    are, aim, ldt, bre, bim, cre, cim, dsk = lay
    n_gb, n_ch_lanes, n_st = bre.shape
    lane_spec = pl.BlockSpec((None, 1, n_st), lambda g: (g, 0, 0))
    mat_spec = pl.BlockSpec((None, n_ch_lanes, n_st), lambda g: (g, 0, 0))
    row_spec = pl.BlockSpec((m, n_ch_lanes), lambda g: (0, g))
    st_spec = pl.BlockSpec((m, n_st), lambda g: (0, g))
    st_shape = jax.ShapeDtypeStruct(h0r.shape, F32)
    return pl.pallas_call(
        _s5_step_kernel,
        grid=(n_gb,),
        in_specs=[lane_spec, lane_spec, lane_spec, mat_spec, mat_spec, mat_spec, mat_spec,
                  pl.BlockSpec((1, n_ch_lanes), lambda g: (0, g)), row_spec, st_spec, st_spec],
        out_specs=[row_spec, st_spec, st_spec],
        out_shape=[jax.ShapeDtypeStruct((m, d_ssm), F32), st_shape, st_shape],
        compiler_params=_params(("parallel",)),
        name="s5_step",
    )(are, aim, ldt, bre, bim, cre, cim, dsk, u, h0r, h0i)


def _s5_layout(a_re, a_im, log_dt, b_re, b_im, c_re, c_im, d_skip):
    n_g, n_st = a_re.shape
    gl = SSM_GROUP_BLOCK
    n_gb = n_g // gl
    same = jnp.eye(gl, dtype=jnp.bool_)[None, :, None, :, None]

    def lanes(a):
        return a.reshape(n_gb, 1, gl * n_st)

    def block_diag(w):
        w5 = w.reshape(n_gb, gl, SSM_GROUP, 1, n_st)
        return jnp.where(same, w5, 0.0).reshape(n_gb, gl * SSM_GROUP, gl * n_st)

    return (lanes(a_re), lanes(a_im), lanes(jnp.broadcast_to(log_dt[:, None], (n_g, n_st))),
            block_diag(b_re.transpose(0, 2, 1)), block_diag(b_im.transpose(0, 2, 1)),
            block_diag(c_re), block_diag(c_im), d_skip.reshape(1, n_g * SSM_GROUP))


def _mix_kernel(x_ref, ys_ref, o_ref, ga_ref, gb_ref, wglu_ref, wso_ref, wao_ref, wo_ref, out_ref):
    ys = ys_ref[...]
    ya = 0.5 * ys * (1.0 + lax.erf(ys * (2.0 ** -0.5)))
    glu = (ya * jax.nn.sigmoid(_dot(ya.astype(BF16), wglu_ref[...]))).astype(BF16)
    ssm = _dot(glu, wso_ref[...])
    att = _dot(o_ref[...], wao_ref[...])
    mix = (ga_ref[...].astype(F32) * ssm + gb_ref[...].astype(F32) * att).astype(BF16)
    out_ref[...] = x_ref[...] + _dot(mix, wo_ref[...])


def _mix(x, ys, o, gates, w_glu, w_ssm_out, w_attn_out, w_o, *, tm):
    m, d = x.shape
    d_ssm = ys.shape[1]
    d_att = o.shape[1]

    def resident(w):
        return pl.BlockSpec(w.shape, lambda i: (0, 0), pipeline_mode=pl.Buffered(1))

    return pl.pallas_call(
        _mix_kernel,
        grid=(m // tm,),
        in_specs=[
            pl.BlockSpec((tm, d), lambda i: (i, 0)),
            pl.BlockSpec((tm, d_ssm), lambda i: (i, 0)),
            pl.BlockSpec((tm, d_att), lambda i: (i, 0)),
            pl.BlockSpec((tm, d), lambda i: (i, 0)),
            pl.BlockSpec((tm, d), lambda i: (i, 1)),
            resident(w_glu), resident(w_ssm_out), resident(w_attn_out), resident(w_o),
        ],
        out_specs=pl.BlockSpec((tm, d), lambda i: (i, 0)),
        out_shape=jax.ShapeDtypeStruct((m, d), F32),
        compiler_params=_params(("parallel",)),
        name="mix",
    )(x, ys, o, gates, gates, w_glu, w_ssm_out, w_attn_out, w_o)


def _rotary_tables(pos, rows):
    half = ROT_DIM // 2
    inv = ROPE_THETA ** (-jnp.arange(half, dtype=F32) / half)
    ang = pos.astype(F32)[:, None] * inv[None, :]
    cos, sin = jnp.cos(ang), jnp.sin(ang)
    t = pos.shape[0]
    zh = jnp.zeros((t, half), F32)
    rest0 = jnp.zeros((t, HEAD_DIM - ROT_DIM), F32)
    cos_c = jnp.concatenate([cos, cos, jnp.ones((t, HEAD_DIM - ROT_DIM), F32)], axis=1)
    s1_c = jnp.concatenate([-sin, zh, rest0], axis=1)
    s2_c = jnp.concatenate([zh, sin, rest0], axis=1)
    return tuple(jnp.broadcast_to(jnp.tile(a, (1, 2)), (rows, 2 * HEAD_DIM)) for a in (cos_c, s1_c, s2_c))


def kernel(x_prompt, x_sample, cache_k, cache_v, state_ssm_re, state_ssm_im, page_table, norm_ffn1, ffn1_w1, ffn1_w3, ffn1_w2, norm_mix, w_in, lam_q1, lam_k1, lam_q2, lam_k2, g_subln, w_attn_out, ssm_a_re, ssm_a_im, ssm_log_dt, ssm_b_re, ssm_b_im, ssm_c_re, ssm_c_im, ssm_d, w_glu, w_ssm_out, w_o, norm_ffn2, ffn2_w1, ffn2_w3, ffn2_w2, norm_final):
    batch, seq, d = x_prompt.shape
    n_dec, dec_seq, _ = x_sample.shape
    assert dec_seq == 1
    depth = cache_k.shape[0]
    n_heads, d_k = cache_k.shape[3], cache_k.shape[4]
    d_v = cache_v.shape[4]
    n_groups, n_state = ssm_a_re.shape[1], ssm_a_re.shape[2]
    d_qk, d_att, d_ssm = n_heads * d_k, n_heads * d_v, n_groups * SSM_GROUP
    past_len = page_table.shape[1] * cache_k.shape[2]
    m_p = batch * seq

    tabs_p = _rotary_tables(jnp.arange(seq, dtype=jnp.int32), seq)
    tabs_s = _rotary_tables(past_len + jnp.arange(dec_seq, dtype=jnp.int32), n_dec)
    row = lambda a: a.reshape(1, -1)
    gfin = row(norm_final)

    h_p = x_prompt.reshape(m_p, d)
    h_s = x_sample.reshape(n_dec, d)
    outs = [[] for _ in range(8)]
    for l in range(depth):
        lam_init = 0.8 - 0.6 * math.exp(-0.3 * l)
        last = l == depth - 1
        w11, w13, w12 = ffn1_w1[l].astype(BF16), ffn1_w3[l].astype(BF16), ffn1_w2[l].astype(BF16)
        w21, w23, w22 = ffn2_w1[l].astype(BF16), ffn2_w3[l].astype(BF16), ffn2_w2[l].astype(BF16)
        win = w_in[l].astype(BF16)
        wglu, wso = w_glu[l].astype(BF16), w_ssm_out[l].astype(BF16)
        wao, wo = w_attn_out[l].astype(BF16), w_o[l].astype(BF16)
        lam4 = jnp.stack([lam_q1[l], lam_k1[l], lam_q2[l], lam_k2[l]])
        gs = row(g_subln[l])
        lay = _s5_layout(ssm_a_re[l], ssm_a_im[l], ssm_log_dt[l], ssm_b_re[l], ssm_b_im[l],
                         ssm_c_re[l], ssm_c_im[l], ssm_d[l])

        def half1(x, tm):
            return _ffn(x, row(norm_ffn1[l]), w11, w13, w12, gfin, final_norm=False, tm=tm, tf=512)

        def half2(x, tm):
            return _ffn(x, row(norm_ffn2[l]), w21, w23, w22, gfin, final_norm=last, tm=tm, tf=512)

        x1_p = half1(h_p, 512)
        q, k, kb, v, vb, u, gates = _project(x1_p, row(norm_mix[l]), win, *tabs_p, tm=512,
                                             d_qk=d_qk, d_v=d_att, d_ssm=d_ssm)
        x1_s = half1(h_s, n_dec)
        q_s, k_s, _, v_s, _, u_s, gates_s = _project(x1_s, row(norm_mix[l]), win, *tabs_s, tm=n_dec,
                                                     d_qk=d_qk, d_v=d_att, d_ssm=d_ssm)

        o, o_s = _attention(q, kb, vb, q_s.astype(F32).reshape(n_dec, n_heads, d_k),
                            k_s.reshape(n_dec, n_heads, d_k), v_s.reshape(n_dec, n_heads, d_v),
                            cache_k, cache_v, page_table, lam4, gs,
                            batch=batch, seq=seq, tq=512, layer=l, lam_init=lam_init)

        ys, hre, him = _s5_prompt(u, lay, batch=batch, seq=seq, rows=1024)
        x2 = _mix(x1_p, ys, o, gates, wglu, wso, wao, wo, tm=256)
        h_p = half2(x2, 512)
        outs[0].append(k.reshape(batch, seq, n_heads, d_k))
        outs[1].append(v.reshape(batch, seq, n_heads, d_v))
        outs[2].append(hre.reshape(batch, n_groups, n_state))
        outs[3].append(him.reshape(batch, n_groups, n_state))

        ys, hre, him = _s5_step(u_s, state_ssm_re[l].reshape(n_dec, n_groups * n_state),
                                state_ssm_im[l].reshape(n_dec, n_groups * n_state), lay)
        x2 = _mix(x1_s, ys, o_s.reshape(n_dec, d_att).astype(BF16), gates_s, wglu, wso, wao, wo, tm=n_dec)
        h_s = half2(x2, n_dec)
        outs[4].append(k_s.reshape(n_dec, dec_seq, n_heads, d_k))
        outs[5].append(v_s.reshape(n_dec, dec_seq, n_heads, d_v))
        outs[6].append(hre.reshape(n_dec, n_groups, n_state))
        outs[7].append(him.reshape(n_dec, n_groups, n_state))

    return (h_p.reshape(batch, seq, d), h_s.reshape(n_dec, dec_seq, d)) + tuple(jnp.stack(o) for o in outs)
```

```python
import functools
import math

import jax
import jax.numpy as jnp
from jax import lax
from jax.experimental import pallas as pl
from jax.experimental.pallas import tpu as pltpu

F32 = jnp.float32
BF16 = jnp.bfloat16

HEAD_DIM = 64
ROT_DIM = HEAD_DIM // 4
ROPE_THETA = 500000.0
EPS = 1e-6
SSM_GROUP = 16
SSM_GROUP_BLOCK = 8

V7X_LANES = 128
V7X_SUBLANES = 8
V7X_MXU_WIDTH = 256
MXU_WIDTH = V7X_MXU_WIDTH
V7X_VMEM_BYTES = 64 * 1024 * 1024
VMEM_LIMIT = V7X_VMEM_BYTES * 3 // 4
ATTN_VMEM_LIMIT = V7X_VMEM_BYTES * 7 // 8

NEG_BIG = -1e30
Q_SCALE = HEAD_DIM ** -0.5 * math.log2(math.e)


def _dot(a, b):
    return jnp.dot(a, b, preferred_element_type=F32)


def _dot_nt(a, b):
    return lax.dot_general(a, b, (((1,), (1,)), ((), ())), preferred_element_type=F32)


def _rms(x, g):
    return x * lax.rsqrt(jnp.mean(x * x, axis=-1, keepdims=True) + EPS) * g


def _params(sem):
    return pltpu.CompilerParams(dimension_semantics=sem, vmem_limit_bytes=VMEM_LIMIT)


def _ffn_kernel(x_ref, g_ref, w1_ref, w3_ref, w2_ref, g2_ref, o_ref, *refs, norm_out, emit_bf16):
    f = pl.program_id(1)
    refs = list(refs)
    xn2_ref = refs.pop(0) if norm_out == "extra" else None
    if emit_bf16:
        w1b_ref, w3b_ref, w2b_ref, xn_s, acc_s = refs
        w1b_ref[...] = w1_ref[...].astype(BF16)
        w3b_ref[...] = w3_ref[...].astype(BF16)
        w2b_ref[...] = w2_ref[...].astype(BF16)
        w1_ref, w3_ref, w2_ref = w1b_ref, w3b_ref, w2b_ref
    else:
        xn_s, acc_s = refs

    @pl.when(f == 0)
    def _():
        xn_s[...] = _rms(x_ref[...], g_ref[...]).astype(BF16)
        acc_s[...] = jnp.zeros_like(acc_s)

    xn = xn_s[...]
    acc = acc_s[...]
    for c in range(w1_ref.shape[1] // MXU_WIDTH):
        cols = slice(c * MXU_WIDTH, (c + 1) * MXU_WIDTH)
        h1 = _dot(xn, w1_ref[:, cols])
        h3 = _dot(xn, w3_ref[:, cols])
        hh = (h1 * jax.nn.sigmoid(h1) * h3).astype(BF16)
        acc = acc + _dot(hh, w2_ref[cols, :])
    acc_s[...] = acc

    @pl.when(f == pl.num_programs(1) - 1)
    def _():
        y = x_ref[...] + 0.5 * acc_s[...]
        if norm_out == "replace":
            y = _rms(y, g2_ref[...])
        elif norm_out == "extra":
            xn2_ref[...] = _rms(y, g2_ref[...]).astype(BF16)
        o_ref[...] = y


def _ffn(x, g, w1, w3, w2, g2, *, norm_out, tm, tf):
    m, d = x.shape
    dff = w1.shape[1]
    emit_bf16 = w1.dtype == F32
    assert not emit_bf16 or m == tm, "each weight tile must be visited exactly once"
    up_spec = pl.BlockSpec((d, tf), lambda i, f: (0, f))
    down_spec = pl.BlockSpec((tf, d), lambda i, f: (f, 0))
    row_spec = pl.BlockSpec((tm, d), lambda i, f: (i, 0))
    gain_spec = pl.BlockSpec((1, d), lambda i, f: (0, 0))
    out_specs = [row_spec]
    out_shape = [jax.ShapeDtypeStruct((m, d), F32)]
    if norm_out == "extra":
        out_specs += [row_spec]
        out_shape += [jax.ShapeDtypeStruct((m, d), BF16)]
    if emit_bf16:
        out_specs += [up_spec, up_spec, down_spec]
        out_shape += [jax.ShapeDtypeStruct(w.shape, BF16) for w in (w1, w3, w2)]
    return pl.pallas_call(
        functools.partial(_ffn_kernel, norm_out=norm_out, emit_bf16=emit_bf16),
        grid=(m // tm, dff // tf),
        in_specs=[row_spec, gain_spec, up_spec, up_spec, down_spec, gain_spec],
        out_specs=out_specs,
        out_shape=out_shape,
        scratch_shapes=[pltpu.VMEM((tm, d), BF16), pltpu.VMEM((tm, d), F32)],
        compiler_params=_params(("parallel", "arbitrary")),
        name="ffn",
    )(x, g, w1, w3, w2, g2)


def _rotate(a, cos_ref, s1_ref, s2_ref):
    cos, s1, s2 = cos_ref[...], s1_ref[...], s2_ref[...]
    half = ROT_DIM // 2
    out = []
    for h in range(a.shape[1] // V7X_LANES):
        blk = a[:, h * V7X_LANES:(h + 1) * V7X_LANES]
        up = pltpu.roll(blk, V7X_LANES - half, 1)
        dn = pltpu.roll(blk, half, 1)
        out.append(blk * cos + up * s1 + dn * s2)
    return jnp.concatenate(out, axis=1)


def _col_chunks(n):
    return [slice(c, c + MXU_WIDTH) for c in range(0, n, MXU_WIDTH)]


def _proj_qk_kernel(xn_ref, w_ref, cos_ref, s1_ref, s2_ref, q_ref, k_ref, kb_ref):
    n = q_ref.shape[1]
    for cols in _col_chunks(n):
        r = _rotate(_dot(xn_ref[...], w_ref[:, cols]), cos_ref, s1_ref, s2_ref)
        q_ref[:, cols] = (r * Q_SCALE).astype(BF16)
    for cols in _col_chunks(n):
        r = _rotate(_dot(xn_ref[...], w_ref[:, slice(n + cols.start, n + cols.stop)]), cos_ref, s1_ref, s2_ref)
        k_ref[:, cols] = r
        kb_ref[:, cols] = r.astype(BF16)


def _proj_vu_kernel(xn_ref, w_ref, v_ref, vb_ref, u_ref):
    n = v_ref.shape[1]
    for cols in _col_chunks(n):
        a = _dot(xn_ref[...], w_ref[:, cols])
        v_ref[:, cols] = a
        vb_ref[:, cols] = a.astype(BF16)
    for cols in _col_chunks(u_ref.shape[1]):
        u_ref[:, cols] = _dot(xn_ref[...], w_ref[:, slice(n + cols.start, n + cols.stop)])


def _proj_gate_kernel(xn_ref, w_ref, s_ref):
    for cols in _col_chunks(s_ref.shape[1]):
        s_ref[:, cols] = jax.nn.sigmoid(_dot(xn_ref[...], w_ref[:, cols])).astype(BF16)


def _project(xn, w_in, cos_t, s1_t, s2_t, *, tm, d_qk, d_v, d_ssm):
    m, d = xn.shape
    tn = 2 * d_qk
    assert d_v + d_ssm == tn and (w_in.shape[1] - 2 * tn) % tn == 0
    n_gate = (w_in.shape[1] - 2 * tn) // tn
    n_t = cos_t.shape[0] // tm
    x_spec = pl.BlockSpec((tm, d), lambda i, j: (i, 0))
    tab_spec = pl.BlockSpec((tm, V7X_LANES), lambda i, j: (i % n_t, 0))

    def row_spec(n):
        return pl.BlockSpec((tm, n), lambda i, j: (i, 0))

    def slab_spec(first):
        return pl.BlockSpec((d, tn), lambda i, j: (0, first + j))

    q, k, kb = pl.pallas_call(
        _proj_qk_kernel,
        grid=(m // tm, 1),
        in_specs=[x_spec, slab_spec(0), tab_spec, tab_spec, tab_spec],
        out_specs=[row_spec(d_qk)] * 3,
        out_shape=[jax.ShapeDtypeStruct((m, d_qk), BF16), jax.ShapeDtypeStruct((m, d_qk), F32),
                   jax.ShapeDtypeStruct((m, d_qk), BF16)],
        compiler_params=_params(("parallel", "arbitrary")),
        name="proj_qk",
    )(xn, w_in, cos_t, s1_t, s2_t)

    v, vb, u = pl.pallas_call(
        _proj_vu_kernel,
        grid=(m // tm, 1),
        in_specs=[x_spec, slab_spec(1)],
        out_specs=[row_spec(d_v), row_spec(d_v), row_spec(d_ssm)],
        out_shape=[jax.ShapeDtypeStruct((m, d_v), F32), jax.ShapeDtypeStruct((m, d_v), BF16),
                   jax.ShapeDtypeStruct((m, d_ssm), F32)],
        compiler_params=_params(("parallel", "arbitrary")),
        name="proj_vu",
    )(xn, w_in)

    gates = pl.pallas_call(
        _proj_gate_kernel,
        grid=(m // tm, n_gate),
        in_specs=[x_spec, slab_spec(2)],
        out_specs=pl.BlockSpec((tm, tn), lambda i, j: (i, j)),
        out_shape=jax.ShapeDtypeStruct((m, n_gate * tn), BF16),
        compiler_params=_params(("parallel", "arbitrary")),
        name="proj_gate",
    )(xn, w_in)
    return q, k, kb, v, vb, u, gates


def _lam(lam_ref, lam_init):
    a = jnp.sum(lam_ref[0:1, :] * lam_ref[1:2, :], axis=-1, keepdims=True)
    b = jnp.sum(lam_ref[2:3, :] * lam_ref[3:4, :], axis=-1, keepdims=True)
    return jnp.exp(a) - jnp.exp(b) + lam_init


def _subln(o, gs, lam_init):
    return _rms(o, gs) * (1.0 - lam_init)


def _split_components(q):
    lane = lax.broadcasted_iota(jnp.int32, q.shape, 1)
    zero = jnp.zeros_like(q)
    return jnp.concatenate([jnp.where(lane < HEAD_DIM, q, zero), jnp.where(lane >= HEAD_DIM, q, zero)], axis=0)


def _softmax_update(s, v_ext, m_s, acc_s):
    m_prev = m_s[...]
    m_new = jnp.maximum(m_prev, jnp.max(s, axis=-1, keepdims=True))
    alpha = jnp.exp2(m_prev - m_new)
    p = jnp.exp2((s - jnp.tile(m_new, (1, s.shape[1] // V7X_LANES))).astype(BF16))
    acc_s[...] = jnp.tile(alpha, (1, 2)) * acc_s[...] + _dot(p, v_ext)
    m_s[...] = m_new


def _prompt_attention(lam_ref, gs_ref, q_ref, k_ref, v_ref, o_ref, qq_s, ve_s, s_s, m_s, acc_s, *, tq, lam_init):
    i = pl.program_id(2)
    dv = v_ref.shape[1]

    @pl.when(i == 0)
    def _():
        ve_s[:, 0:dv] = v_ref[...]
        ve_s[:, dv:] = jnp.ones((ve_s.shape[0], ve_s.shape[1] - dv), BF16)

    qq_s[...] = _split_components(q_ref[...])
    m_s[...] = jnp.full_like(m_s, NEG_BIG)
    acc_s[...] = jnp.zeros_like(acc_s)

    def tile(ref, j):
        return ref[pl.ds(pl.multiple_of(j * tq, tq), tq), :]

    def scores(j, slot):
        s_s[slot] = _dot_nt(qq_s[...], tile(k_ref, j))

    def consume(j, slot, mask=None):
        s = s_s[slot]
        if mask is not None:
            s = jnp.where(mask, s, NEG_BIG)
        _softmax_update(s, tile(ve_s, j), m_s, acc_s)

    scores(0, 0)

    def pair(jj, c):
        j = 2 * jj
        scores(j + 1, 1)
        consume(j, 0)
        scores(j + 2, 0)
        consume(j + 1, 1)
        return c

    lax.fori_loop(0, i // 2, pair, 0)

    @pl.when(i % 2 == 1)
    def _():
        scores(i, 1)
        consume(i - 1, 0)
        s_s[0] = s_s[1]

    row = lax.broadcasted_iota(jnp.int32, s_s.shape[1:], 0)
    row = jnp.where(row >= tq, row - tq, row)
    col = lax.broadcasted_iota(jnp.int32, s_s.shape[1:], 1)
    consume(i, 0, mask=col <= row)

    o1 = acc_s[0:tq, 0:dv] / acc_s[0:tq, dv:]
    o2 = acc_s[tq:2 * tq, 0:dv] / acc_s[tq:2 * tq, dv:]
    o = o1 - _lam(lam_ref, lam_init) * o2
    o_ref[...] = _subln(o, gs_ref[...], lam_init).astype(BF16)


def _decode_attention(lam_ref, gs_ref, q_ref, kn_ref, vn_ref, kc_refs, vc_refs, o_ref, *, lam_init):
    n_heads = q_ref.shape[0]
    page, _, dk = kc_refs[0].shape
    dv = vc_refs[0].shape[2]

    qm = _split_components(q_ref[...])
    qmb = qm.astype(BF16)
    shape = (2 * n_heads, page * n_heads)
    row = lax.broadcasted_iota(jnp.int32, shape, 0)
    col = lax.broadcasted_iota(jnp.int32, shape, 1)
    same_head = ((row ^ col) & (n_heads - 1)) == 0

    kn2 = jnp.concatenate([kn_ref[...], kn_ref[...]], axis=0)
    vn2 = jnp.concatenate([vn_ref[...], vn_ref[...]], axis=0)
    s_self = jnp.sum(qm * kn2, axis=-1, keepdims=True)

    scores = []
    m = s_self
    for kc_ref in kc_refs:
        k2 = kc_ref[...].reshape(page * n_heads, dk).astype(BF16)
        s = jnp.where(same_head, _dot_nt(qmb, k2), NEG_BIG)
        scores.append(s)
        m = jnp.maximum(m, jnp.max(s, axis=-1, keepdims=True))

    p_self = jnp.exp2(s_self - m)
    l = p_self
    acc = p_self * vn2
    for s, vc_ref in zip(scores, vc_refs):
        p = jnp.exp2(s - m)
        l = l + jnp.sum(p, axis=-1, keepdims=True)
        acc = acc + _dot(p.astype(BF16), vc_ref[...].reshape(page * n_heads, dv).astype(BF16))

    o = acc / l
    o = o[0:n_heads, :] - _lam(lam_ref, lam_init) * o[n_heads:2 * n_heads, :]
    o_ref[...] = _subln(o, gs_ref[...], lam_init)


def _attn_kernel(pt_ref, lam_ref, gs_ref, q_ref, k_ref, v_ref, qd_ref, knd_ref, vnd_ref, *refs,
                 n_pages, tq, lam_init):
    del pt_ref
    kc_refs, vc_refs = refs[:n_pages], refs[n_pages:2 * n_pages]
    o_ref, od_ref, qq_s, ve_s, s_s, m_s, acc_s = refs[2 * n_pages:]
    _decode_attention(lam_ref, gs_ref, qd_ref, knd_ref, vnd_ref, kc_refs, vc_refs, od_ref, lam_init=lam_init)
    _prompt_attention(lam_ref, gs_ref, q_ref, k_ref, v_ref, o_ref, qq_s, ve_s, s_s, m_s, acc_s,
                      tq=tq, lam_init=lam_init)


def _attention(qb, kb, vb, q3, kn3, vn3, cache_k, cache_v, page_table, lam4, gs, *, batch, seq, tq, layer, lam_init):
    m, dq = qb.shape
    n_heads = dq // V7X_LANES
    nq = seq // tq
    n_dec, _, dk = q3.shape
    n_pages = page_table.shape[1]
    page = cache_k.shape[2]
    dv = cache_v.shape[4]
    assert n_heads & (n_heads - 1) == 0 and dv == V7X_LANES
    assert n_dec == batch * n_heads * nq, "one decode sample per prompt-attention grid step"

    def sample(b, h, i):
        return (b * n_heads + h) * nq + i

    def sample_spec(d):
        return pl.BlockSpec((None, n_heads, d), lambda b, h, i, pt: (sample(b, h, i), 0, 0))

    def page_specs(d):
        return [pl.BlockSpec((None, None, page, n_heads, d),
                             lambda b, h, i, pt, p=p: (layer, pt[sample(b, h, i), p], 0, 0, 0))
                for p in range(n_pages)]

    tile_spec = pl.BlockSpec((tq, V7X_LANES), lambda b, h, i, pt: (b * nq + i, h))
    head_spec = pl.BlockSpec((seq, V7X_LANES), lambda b, h, i, pt: (b, h))
    grid_spec = pltpu.PrefetchScalarGridSpec(
        num_scalar_prefetch=1,
        grid=(batch, n_heads, nq),
        in_specs=[
            pl.BlockSpec(lam4.shape, lambda b, h, i, pt: (0, 0)),
            pl.BlockSpec((1, dv), lambda b, h, i, pt: (0, 0)),
            tile_spec, head_spec, head_spec,
            sample_spec(dk), sample_spec(dk), sample_spec(dv),
            *page_specs(dk), *page_specs(dv),
        ],
        out_specs=[tile_spec, sample_spec(dv)],
        scratch_shapes=[pltpu.VMEM((2 * tq, V7X_LANES), BF16), pltpu.VMEM((seq, 2 * V7X_LANES), BF16),
                        pltpu.VMEM((2, 2 * tq, tq), F32), pltpu.VMEM((2 * tq, V7X_LANES), F32),
                        pltpu.VMEM((2 * tq, 2 * V7X_LANES), F32)],
    )
    return pl.pallas_call(
        functools.partial(_attn_kernel, n_pages=n_pages, tq=tq, lam_init=lam_init),
        grid_spec=grid_spec,
        out_shape=[jax.ShapeDtypeStruct((m, dq), BF16), jax.ShapeDtypeStruct((n_dec, n_heads, dv), F32)],
        compiler_params=pltpu.CompilerParams(dimension_semantics=("arbitrary",) * 3,
                                             vmem_limit_bytes=ATTN_VMEM_LIMIT),
        name="attention",
    )(page_table, lam4, gs, qb, kb, vb, q3, kn3, vn3, *([cache_k] * n_pages), *([cache_v] * n_pages))


def _s5_discretize(are, aim, ldt):
    dt = jnp.exp(ldt)
    mag = jnp.exp(dt * are)
    lbr = mag * jnp.cos(dt * aim)
    lbi = mag * jnp.sin(dt * aim)
    nr = lbr - 1.0
    den = are * are + aim * aim
    cr = (nr * are + lbi * aim) / den
    ci = (lbi * are - nr * aim) / den
    return lbr, lbi, cr, ci


def _s5_input_weight(cr, ci, bre, bim):
    return jnp.concatenate([cr * bre - ci * bim, cr * bim + ci * bre], axis=1).astype(BF16)


def _s5_readout(hr, hi, cre_ref, cim_ref, d_ref, u):
    return (_dot_nt(hr.astype(BF16), cre_ref[...].astype(BF16))
            - _dot_nt(hi.astype(BF16), cim_ref[...].astype(BF16)) + d_ref[...] * u)


def _s5_prompt_kernel(are_ref, aim_ref, ldt_ref, bre_ref, bim_ref, cre_ref, cim_ref, d_ref, u_ref,
                      y_ref, hre_ref, him_ref, up_s, bur_s, bui_s, yp_s, wb_s, lam_s, car_s, *, seg):
    ch = pl.program_id(2)
    n_st = bur_s.shape[1]
    sub = V7X_SUBLANES

    @pl.when(ch == 0)
    def _():
        lbr, lbi, cr, ci = _s5_discretize(are_ref[...], aim_ref[...], ldt_ref[...])
        wb_s[...] = _s5_input_weight(cr, ci, bre_ref[...], bim_ref[...])
        lam_s[0:1, :] = lbr
        lam_s[1:2, :] = lbi
        pr, pi = lbr, lbi
        for _ in range(int(math.log2(seg))):
            pr, pi = pr * pr - pi * pi, 2.0 * pr * pi
        lam_s[2:3, :] = pr
        lam_s[3:4, :] = pi
        car_s[...] = jnp.zeros_like(car_s)

    def permute(t, c):
        up_s[pl.ds(pl.multiple_of(t * sub, sub), sub), :] = u_ref[pl.ds(t, sub, stride=seg), :]
        return c

    lax.fori_loop(0, seg, permute, 0, unroll=8)

    bu = _dot(up_s[...].astype(BF16), wb_s[...])
    bur_s[...] = bu[:, :n_st]
    bui_s[...] = bu[:, n_st:]

    lbr = jnp.broadcast_to(lam_s[0:1, :], (sub, n_st))
    lbi = jnp.broadcast_to(lam_s[1:2, :], (sub, n_st))

    def advance(t, c):
        sr, si = c
        r0 = pl.multiple_of(t * sub, sub)
        return (lbr * sr - lbi * si + bur_s[pl.ds(r0, sub), :],
                lbr * si + lbi * sr + bui_s[pl.ds(r0, sub), :])

    zero = jnp.zeros((sub, n_st), F32)
    er, ei = lax.fori_loop(0, seg, advance, (zero, zero), unroll=4)

    psr, psi = lam_s[2:3, :], lam_s[3:4, :]
    hr, hi = car_s[0:1, :], car_s[1:2, :]
    rows_r, rows_i = [], []
    for r in range(sub):
        rows_r.append(hr)
        rows_i.append(hi)
        hr, hi = (er[r:r + 1, :] + psr * hr - psi * hi, ei[r:r + 1, :] + psr * hi + psi * hr)
    car_s[0:1, :] = hr
    car_s[1:2, :] = hi

    def advance_store(t, c):
        sr, si = advance(t, c)
        r0 = pl.multiple_of(t * sub, sub)
        bur_s[pl.ds(r0, sub), :] = sr
        bui_s[pl.ds(r0, sub), :] = si
        return sr, si

    lax.fori_loop(0, seg, advance_store,
                  (jnp.concatenate(rows_r, axis=0), jnp.concatenate(rows_i, axis=0)), unroll=4)

    yp_s[...] = _s5_readout(bur_s[...], bui_s[...], cre_ref, cim_ref, d_ref, up_s[...])

    def unpermute(t, c):
        y_ref[pl.ds(t, sub, stride=seg), :] = yp_s[pl.ds(pl.multiple_of(t * sub, sub), sub), :]
        return c

    lax.fori_loop(0, seg, unpermute, 0, unroll=8)

    @pl.when(ch == pl.num_programs(2) - 1)
    def _():
        hre_ref[...] = hr
        him_ref[...] = hi


def _s5_prompt(u, lay, *, batch, seq, rows):
    m, d_ssm = u.shape
    are, aim, ldt, bre, bim, cre, cim, dsk = lay
    n_gb, n_ch_lanes, n_st = bre.shape
    n_ch = seq // rows
    seg = rows // V7X_SUBLANES
    lane_spec = pl.BlockSpec((None, 1, n_st), lambda b, g, c: (g, 0, 0))
    mat_spec = pl.BlockSpec((None, n_ch_lanes, n_st), lambda b, g, c: (g, 0, 0))
    row_spec = pl.BlockSpec((rows, n_ch_lanes), lambda b, g, c: (b * n_ch + c, g))
    st_spec = pl.BlockSpec((None, None, 1, n_st), lambda b, g, c: (b, g, 0, 0))
    st_shape = jax.ShapeDtypeStruct((batch, n_gb, 1, n_st), F32)
    return pl.pallas_call(
        functools.partial(_s5_prompt_kernel, seg=seg),
        grid=(batch, n_gb, n_ch),
        in_specs=[lane_spec, lane_spec, lane_spec, mat_spec, mat_spec, mat_spec, mat_spec,
                  pl.BlockSpec((1, n_ch_lanes), lambda b, g, c: (0, g)), row_spec],
        out_specs=[row_spec, st_spec, st_spec],
        out_shape=[jax.ShapeDtypeStruct((m, d_ssm), F32), st_shape, st_shape],
        scratch_shapes=[pltpu.VMEM((rows, n_ch_lanes), F32), pltpu.VMEM((rows, n_st), F32),
                        pltpu.VMEM((rows, n_st), F32), pltpu.VMEM((rows, n_ch_lanes), F32),
                        pltpu.VMEM((n_ch_lanes, 2 * n_st), BF16), pltpu.VMEM((4, n_st), F32),
                        pltpu.VMEM((2, n_st), F32)],
        compiler_params=_params(("parallel", "parallel", "arbitrary")),
        name="s5_prompt",
    )(are, aim, ldt, bre, bim, cre, cim, dsk, u)


def _s5_step_kernel(are_ref, aim_ref, ldt_ref, bre_ref, bim_ref, cre_ref, cim_ref, d_ref, u_ref,
                    h0r_ref, h0i_ref, y_ref, hre_ref, him_ref):
    n_st = h0r_ref.shape[1]
    lbr, lbi, cr, ci = _s5_discretize(are_ref[...], aim_ref[...], ldt_ref[...])
    u = u_ref[...]
    bu = _dot(u.astype(BF16), _s5_input_weight(cr, ci, bre_ref[...], bim_ref[...]))
    h0r, h0i = h0r_ref[...], h0i_ref[...]
    hr = lbr * h0r - lbi * h0i + bu[:, :n_st]
    hi = lbr * h0i + lbi * h0r + bu[:, n_st:]
    hre_ref[...] = hr
    him_ref[...] = hi
    y_ref[...] = _s5_readout(hr, hi, cre_ref, cim_ref, d_ref, u)


def _s5_step(u, h0r, h0i, lay):
    m, d_ssm = u.shape
    are, aim, ldt, bre, bim, cre, cim, dsk = lay
    n_gb, n_ch_lanes, n_st = bre.shape
    lane_spec = pl.BlockSpec((None, 1, n_st), lambda g: (g, 0, 0))
    mat_spec = pl.BlockSpec((None, n_ch_lanes, n_st), lambda g: (g, 0, 0))
    row_spec = pl.BlockSpec((m, n_ch_lanes), lambda g: (0, g))
    st_spec = pl.BlockSpec((m, n_st), lambda g: (0, g))
    st_shape = jax.ShapeDtypeStruct(h0r.shape, F32)
    return pl.pallas_call(
        _s5_step_kernel,
        grid=(n_gb,),
        in_specs=[lane_spec, lane_spec, lane_spec, mat_spec, mat_spec, mat_spec, mat_spec,
                  pl.BlockSpec((1, n_ch_lanes), lambda g: (0, g)), row_spec, st_spec, st_spec],
        out_specs=[row_spec, st_spec, st_spec],
        out_shape=[jax.ShapeDtypeStruct((m, d_ssm), F32), st_shape, st_shape],
        compiler_params=_params(("parallel",)),
        name="s5_step",
    )(are, aim, ldt, bre, bim, cre, cim, dsk, u, h0r, h0i)


def _s5_layout(a_re, a_im, log_dt, b_re, b_im, c_re, c_im, d_skip):
    n_g, n_st = a_re.shape
    gl = SSM_GROUP_BLOCK
    n_gb = n_g // gl
    same = jnp.eye(gl, dtype=jnp.bool_)[None, :, None, :, None]

    def lanes(a):
        return a.reshape(n_gb, 1, gl * n_st)

    def block_diag(w):
        w5 = w.reshape(n_gb, gl, SSM_GROUP, 1, n_st)
        return jnp.where(same, w5, 0.0).reshape(n_gb, gl * SSM_GROUP, gl * n_st)

    return (lanes(a_re), lanes(a_im), lanes(jnp.broadcast_to(log_dt[:, None], (n_g, n_st))),
            block_diag(b_re.transpose(0, 2, 1)), block_diag(b_im.transpose(0, 2, 1)),
            block_diag(c_re), block_diag(c_im), d_skip.reshape(1, n_g * SSM_GROUP))


def _mix_kernel(x_ref, ys_ref, o_ref, ga_ref, gb_ref, wglu_ref, wso_ref, wao_ref, wo_ref, out_ref):
    ys = ys_ref[...]
    ya = 0.5 * ys * (1.0 + lax.erf(ys * (2.0 ** -0.5)))
    glu = (ya * jax.nn.sigmoid(_dot(ya.astype(BF16), wglu_ref[...]))).astype(BF16)
    ssm = _dot(glu, wso_ref[...])
    att = _dot(o_ref[...], wao_ref[...])
    mix = (ga_ref[...].astype(F32) * ssm + gb_ref[...].astype(F32) * att).astype(BF16)
    out_ref[...] = x_ref[...] + _dot(mix, wo_ref[...])


def _mix(x, ys, o, gates, w_glu, w_ssm_out, w_attn_out, w_o, *, tm):
    m, d = x.shape
    d_ssm = ys.shape[1]
    d_att = o.shape[1]

    def resident(w):
        return pl.BlockSpec(w.shape, lambda i: (0, 0), pipeline_mode=pl.Buffered(1))

    return pl.pallas_call(
        _mix_kernel,
        grid=(m // tm,),
        in_specs=[
            pl.BlockSpec((tm, d), lambda i: (i, 0)),
            pl.BlockSpec((tm, d_ssm), lambda i: (i, 0)),
            pl.BlockSpec((tm, d_att), lambda i: (i, 0)),
            pl.BlockSpec((tm, d), lambda i: (i, 0)),
            pl.BlockSpec((tm, d), lambda i: (i, 1)),
            resident(w_glu), resident(w_ssm_out), resident(w_attn_out), resident(w_o),
        ],
        out_specs=pl.BlockSpec((tm, d), lambda i: (i, 0)),
        out_shape=jax.ShapeDtypeStruct((m, d), F32),
        compiler_params=_params(("parallel",)),
        name="mix",
    )(x, ys, o, gates, gates, w_glu, w_ssm_out, w_attn_out, w_o)


def _rotary_tables(pos, rows):
    half = ROT_DIM // 2
    inv = ROPE_THETA ** (-jnp.arange(half, dtype=F32) / half)
    ang = pos.astype(F32)[:, None] * inv[None, :]
    cos, sin = jnp.cos(ang), jnp.sin(ang)
    t = pos.shape[0]
    zh = jnp.zeros((t, half), F32)
    rest0 = jnp.zeros((t, HEAD_DIM - ROT_DIM), F32)
    cos_c = jnp.concatenate([cos, cos, jnp.ones((t, HEAD_DIM - ROT_DIM), F32)], axis=1)
    s1_c = jnp.concatenate([-sin, zh, rest0], axis=1)
    s2_c = jnp.concatenate([zh, sin, rest0], axis=1)
    return tuple(jnp.broadcast_to(jnp.tile(a, (1, 2)), (rows, 2 * HEAD_DIM)) for a in (cos_c, s1_c, s2_c))


def kernel(x_prompt, x_sample, cache_k, cache_v, state_ssm_re, state_ssm_im, page_table, norm_ffn1, ffn1_w1, ffn1_w3, ffn1_w2, norm_mix, w_in, lam_q1, lam_k1, lam_q2, lam_k2, g_subln, w_attn_out, ssm_a_re, ssm_a_im, ssm_log_dt, ssm_b_re, ssm_b_im, ssm_c_re, ssm_c_im, ssm_d, w_glu, w_ssm_out, w_o, norm_ffn2, ffn2_w1, ffn2_w3, ffn2_w2, norm_final):
    batch, seq, d = x_prompt.shape
    n_dec, dec_seq, _ = x_sample.shape
    assert dec_seq == 1
    depth = cache_k.shape[0]
    n_heads, d_k = cache_k.shape[3], cache_k.shape[4]
    d_v = cache_v.shape[4]
    n_groups, n_state = ssm_a_re.shape[1], ssm_a_re.shape[2]
    d_qk, d_att, d_ssm = n_heads * d_k, n_heads * d_v, n_groups * SSM_GROUP
    past_len = page_table.shape[1] * cache_k.shape[2]
    m_p = batch * seq

    tabs_p = _rotary_tables(jnp.arange(seq, dtype=jnp.int32), seq)
    tabs_s = _rotary_tables(past_len + jnp.arange(dec_seq, dtype=jnp.int32), n_dec)
    row = lambda a: a.reshape(1, -1)
    gfin = row(norm_final)

    h_p = x_prompt.reshape(m_p, d)
    h_s = x_sample.reshape(n_dec, d)
    outs = [[] for _ in range(8)]
    for l in range(depth):
        lam_init = 0.8 - 0.6 * math.exp(-0.3 * l)
        last = l == depth - 1
        win = w_in[l].astype(BF16)
        wglu, wso = w_glu[l].astype(BF16), w_ssm_out[l].astype(BF16)
        wao, wo = w_attn_out[l].astype(BF16), w_o[l].astype(BF16)
        lam4 = jnp.stack([lam_q1[l], lam_k1[l], lam_q2[l], lam_k2[l]])
        gs = row(g_subln[l])
        lay = _s5_layout(ssm_a_re[l], ssm_a_im[l], ssm_log_dt[l], ssm_b_re[l], ssm_b_im[l],
                         ssm_c_re[l], ssm_c_im[l], ssm_d[l])

        def half1(x, tm, w):
            return _ffn(x, row(norm_ffn1[l]), *w, row(norm_mix[l]), norm_out="extra", tm=tm, tf=512)

        def half2(x, tm, w):
            return _ffn(x, row(norm_ffn2[l]), *w, gfin, norm_out="replace" if last else "none", tm=tm, tf=512)

        x1_s, xn_s, *w1_bf = half1(h_s, n_dec, (ffn1_w1[l], ffn1_w3[l], ffn1_w2[l]))
        q_s, k_s, _, v_s, _, u_s, gates_s = _project(xn_s, win, *tabs_s, tm=n_dec,
                                                     d_qk=d_qk, d_v=d_att, d_ssm=d_ssm)
        x1_p, xn_p = half1(h_p, 512, w1_bf)
        q, k, kb, v, vb, u, gates = _project(xn_p, win, *tabs_p, tm=1024,
                                             d_qk=d_qk, d_v=d_att, d_ssm=d_ssm)

        o, o_s = _attention(q, kb, vb, q_s.astype(F32).reshape(n_dec, n_heads, d_k),
                            k_s.reshape(n_dec, n_heads, d_k), v_s.reshape(n_dec, n_heads, d_v),
                            cache_k, cache_v, page_table, lam4, gs,
                            batch=batch, seq=seq, tq=512, layer=l, lam_init=lam_init)

        ys, hre, him = _s5_step(u_s, state_ssm_re[l].reshape(n_dec, n_groups * n_state),
                                state_ssm_im[l].reshape(n_dec, n_groups * n_state), lay)
        x2 = _mix(x1_s, ys, o_s.reshape(n_dec, d_att).astype(BF16), gates_s, wglu, wso, wao, wo, tm=n_dec)
        h_s, *w2_bf = half2(x2, n_dec, (ffn2_w1[l], ffn2_w3[l], ffn2_w2[l]))
        outs[4].append(k_s.reshape(n_dec, dec_seq, n_heads, d_k))
        outs[5].append(v_s.reshape(n_dec, dec_seq, n_heads, d_v))
        outs[6].append(hre.reshape(n_dec, n_groups, n_state))
        outs[7].append(him.reshape(n_dec, n_groups, n_state))

        ys, hre, him = _s5_prompt(u, lay, batch=batch, seq=seq, rows=1024)
        x2 = _mix(x1_p, ys, o, gates, wglu, wso, wao, wo, tm=256)
        (h_p,) = half2(x2, 512, w2_bf)
        outs[0].append(k.reshape(batch, seq, n_heads, d_k))
        outs[1].append(v.reshape(batch, seq, n_heads, d_v))
        outs[2].append(hre.reshape(batch, n_groups, n_state))
        outs[3].append(him.reshape(batch, n_groups, n_state))

    return (h_p.reshape(batch, seq, d), h_s.reshape(n_dec, dec_seq, d)) + tuple(jnp.stack(o) for o in outs)
```

```python
import functools
import math

import jax
import jax.numpy as jnp
from jax import lax
from jax.experimental import pallas as pl
from jax.experimental.pallas import tpu as pltpu

F32 = jnp.float32
BF16 = jnp.bfloat16

HEAD_DIM = 64
ROT_DIM = HEAD_DIM // 4
ROPE_THETA = 500000.0
EPS = 1e-6
SSM_GROUP = 16
SSM_GROUP_BLOCK = 8

V7X_LANES = 128
V7X_SUBLANES = 8
V7X_MXU_WIDTH = 256
MXU_WIDTH = V7X_MXU_WIDTH
V7X_VMEM_BYTES = 64 * 1024 * 1024
VMEM_LIMIT = V7X_VMEM_BYTES * 3 // 4
ATTN_VMEM_LIMIT = V7X_VMEM_BYTES * 7 // 8

NEG_BIG = -1e30
Q_SCALE = HEAD_DIM ** -0.5 * math.log2(math.e)


def _dot(a, b):
    return jnp.dot(a, b, preferred_element_type=F32)


def _dot_nt(a, b):
    return lax.dot_general(a, b, (((1,), (1,)), ((), ())), preferred_element_type=F32)


def _rms(x, g):
    return x * lax.rsqrt(jnp.mean(x * x, axis=-1, keepdims=True) + EPS) * g


def _params(sem):
    return pltpu.CompilerParams(dimension_semantics=sem, vmem_limit_bytes=VMEM_LIMIT)


def _ffn_kernel(x_ref, g_ref, w1_ref, w3_ref, w2_ref, g2_ref, o_ref, *refs, norm_out, emit_bf16):
    f = pl.program_id(1)
    refs = list(refs)
    xn2_ref = refs.pop(0) if norm_out == "extra" else None
    if emit_bf16:
        w1b_ref, w3b_ref, w2b_ref, xn_s, acc_s = refs
        w1b_ref[...] = w1_ref[...].astype(BF16)
        w3b_ref[...] = w3_ref[...].astype(BF16)
        w2b_ref[...] = w2_ref[...].astype(BF16)
        w1_ref, w3_ref, w2_ref = w1b_ref, w3b_ref, w2b_ref
    else:
        xn_s, acc_s = refs

    @pl.when(f == 0)
    def _():
        xn_s[...] = _rms(x_ref[...], g_ref[...]).astype(BF16)
        acc_s[...] = jnp.zeros_like(acc_s)

    xn = xn_s[...]
    acc = acc_s[...]
    for c in range(w1_ref.shape[1] // MXU_WIDTH):
        cols = slice(c * MXU_WIDTH, (c + 1) * MXU_WIDTH)
        h1 = _dot(xn, w1_ref[:, cols])
        h3 = _dot(xn, w3_ref[:, cols])
        hh = (h1 * jax.nn.sigmoid(h1) * h3).astype(BF16)
        acc = acc + _dot(hh, w2_ref[cols, :])
    acc_s[...] = acc

    @pl.when(f == pl.num_programs(1) - 1)
    def _():
        y = x_ref[...] + 0.5 * acc_s[...]
        if norm_out == "replace":
            y = _rms(y, g2_ref[...])
        elif norm_out == "extra":
            xn2_ref[...] = _rms(y, g2_ref[...]).astype(BF16)
        o_ref[...] = y


def _ffn(x, g, w1, w3, w2, g2, *, norm_out, tm, tf):
    m, d = x.shape
    dff = w1.shape[1]
    emit_bf16 = w1.dtype == F32
    assert not emit_bf16 or m == tm, "each weight tile must be visited exactly once"
    up_spec = pl.BlockSpec((d, tf), lambda i, f: (0, f))
    down_spec = pl.BlockSpec((tf, d), lambda i, f: (f, 0))
    row_spec = pl.BlockSpec((tm, d), lambda i, f: (i, 0))
    gain_spec = pl.BlockSpec((1, d), lambda i, f: (0, 0))
    out_specs = [row_spec]
    out_shape = [jax.ShapeDtypeStruct((m, d), F32)]
    if norm_out == "extra":
        out_specs += [row_spec]
        out_shape += [jax.ShapeDtypeStruct((m, d), BF16)]
    if emit_bf16:
        out_specs += [up_spec, up_spec, down_spec]
        out_shape += [jax.ShapeDtypeStruct(w.shape, BF16) for w in (w1, w3, w2)]
    return pl.pallas_call(
        functools.partial(_ffn_kernel, norm_out=norm_out, emit_bf16=emit_bf16),
        grid=(m // tm, dff // tf),
        in_specs=[row_spec, gain_spec, up_spec, up_spec, down_spec, gain_spec],
        out_specs=out_specs,
        out_shape=out_shape,
        scratch_shapes=[pltpu.VMEM((tm, d), BF16), pltpu.VMEM((tm, d), F32)],
        compiler_params=_params(("parallel", "arbitrary")),
        name="ffn",
    )(x, g, w1, w3, w2, g2)


def _rotate(a, cos_ref, s1_ref, s2_ref):
    cos, s1, s2 = cos_ref[...], s1_ref[...], s2_ref[...]
    half = ROT_DIM // 2
    out = []
    for h in range(a.shape[1] // V7X_LANES):
        blk = a[:, h * V7X_LANES:(h + 1) * V7X_LANES]
        up = pltpu.roll(blk, V7X_LANES - half, 1)
        dn = pltpu.roll(blk, half, 1)
        out.append(blk * cos + up * s1 + dn * s2)
    return jnp.concatenate(out, axis=1)


def _col_chunks(n):
    return [slice(c, c + MXU_WIDTH) for c in range(0, n, MXU_WIDTH)]


def _proj_qk_kernel(xn_ref, w_ref, cos_ref, s1_ref, s2_ref, q_ref, k_ref, kb_ref):
    n = q_ref.shape[1]
    for cols in _col_chunks(n):
        r = _rotate(_dot(xn_ref[...], w_ref[:, cols]), cos_ref, s1_ref, s2_ref)
        q_ref[:, cols] = (r * Q_SCALE).astype(BF16)
    for cols in _col_chunks(n):
        r = _rotate(_dot(xn_ref[...], w_ref[:, slice(n + cols.start, n + cols.stop)]), cos_ref, s1_ref, s2_ref)
        k_ref[:, cols] = r
        kb_ref[:, cols] = r.astype(BF16)


def _proj_vu_kernel(xn_ref, w_ref, v_ref, vb_ref, u_ref):
    n = v_ref.shape[1]
    for cols in _col_chunks(n):
        a = _dot(xn_ref[...], w_ref[:, cols])
        v_ref[:, cols] = a
        vb_ref[:, cols] = a.astype(BF16)
    for cols in _col_chunks(u_ref.shape[1]):
        u_ref[:, cols] = _dot(xn_ref[...], w_ref[:, slice(n + cols.start, n + cols.stop)])


def _proj_gate_kernel(xn_ref, w_ref, s_ref):
    for cols in _col_chunks(s_ref.shape[1]):
        s_ref[:, cols] = jax.nn.sigmoid(_dot(xn_ref[...], w_ref[:, cols])).astype(BF16)


def _project(xn, w_in, cos_t, s1_t, s2_t, *, tm, d_qk, d_v, d_ssm):
    m, d = xn.shape
    tn = 2 * d_qk
    assert d_v + d_ssm == tn and (w_in.shape[1] - 2 * tn) % tn == 0
    n_gate = (w_in.shape[1] - 2 * tn) // tn
    n_t = cos_t.shape[0] // tm
    x_spec = pl.BlockSpec((tm, d), lambda i, j: (i, 0))
    tab_spec = pl.BlockSpec((tm, V7X_LANES), lambda i, j: (i % n_t, 0))

    def row_spec(n):
        return pl.BlockSpec((tm, n), lambda i, j: (i, 0))

    def slab_spec(first):
        return pl.BlockSpec((d, tn), lambda i, j: (0, first + j))

    q, k, kb = pl.pallas_call(
        _proj_qk_kernel,
        grid=(m // tm, 1),
        in_specs=[x_spec, slab_spec(0), tab_spec, tab_spec, tab_spec],
        out_specs=[row_spec(d_qk)] * 3,
        out_shape=[jax.ShapeDtypeStruct((m, d_qk), BF16), jax.ShapeDtypeStruct((m, d_qk), F32),
                   jax.ShapeDtypeStruct((m, d_qk), BF16)],
        compiler_params=_params(("parallel", "arbitrary")),
        name="proj_qk",
    )(xn, w_in, cos_t, s1_t, s2_t)

    v, vb, u = pl.pallas_call(
        _proj_vu_kernel,
        grid=(m // tm, 1),
        in_specs=[x_spec, slab_spec(1)],
        out_specs=[row_spec(d_v), row_spec(d_v), row_spec(d_ssm)],
        out_shape=[jax.ShapeDtypeStruct((m, d_v), F32), jax.ShapeDtypeStruct((m, d_v), BF16),
                   jax.ShapeDtypeStruct((m, d_ssm), F32)],
        compiler_params=_params(("parallel", "arbitrary")),
        name="proj_vu",
    )(xn, w_in)

    gates = pl.pallas_call(
        _proj_gate_kernel,
        grid=(m // tm, n_gate),
        in_specs=[x_spec, slab_spec(2)],
        out_specs=pl.BlockSpec((tm, tn), lambda i, j: (i, j)),
        out_shape=jax.ShapeDtypeStruct((m, n_gate * tn), BF16),
        compiler_params=_params(("parallel", "arbitrary")),
        name="proj_gate",
    )(xn, w_in)
    return q, k, kb, v, vb, u, gates


def _lam(lam_ref, lam_init):
    a = jnp.sum(lam_ref[0:1, :] * lam_ref[1:2, :], axis=-1, keepdims=True)
    b = jnp.sum(lam_ref[2:3, :] * lam_ref[3:4, :], axis=-1, keepdims=True)
    return jnp.exp(a) - jnp.exp(b) + lam_init


def _subln(o, gs, lam_init):
    return _rms(o, gs) * (1.0 - lam_init)


def _split_components(q):
    lane = lax.broadcasted_iota(jnp.int32, q.shape, 1)
    zero = jnp.zeros_like(q)
    return jnp.concatenate([jnp.where(lane < HEAD_DIM, q, zero), jnp.where(lane >= HEAD_DIM, q, zero)], axis=0)


def _softmax_update(s, v_ext, m_s, acc_s):
    m_prev = m_s[...]
    m_new = jnp.maximum(m_prev, jnp.max(s, axis=-1, keepdims=True))
    alpha = jnp.exp2(m_prev - m_new)
    p = jnp.exp2((s - jnp.tile(m_new, (1, s.shape[1] // V7X_LANES))).astype(BF16))
    acc_s[...] = jnp.tile(alpha, (1, 2)) * acc_s[...] + _dot(p, v_ext)
    m_s[...] = m_new


def _prompt_attention(lam_ref, gs_ref, q_ref, k_ref, v_ref, o_ref, qq_s, ve_s, s_s, m_s, acc_s, *, tq, lam_init):
    i = pl.program_id(2)
    dv = v_ref.shape[1]

    @pl.when(i == 0)
    def _():
        ve_s[:, 0:dv] = v_ref[...]
        ve_s[:, dv:] = jnp.ones((ve_s.shape[0], ve_s.shape[1] - dv), BF16)

    qq_s[...] = _split_components(q_ref[...])
    m_s[...] = jnp.full_like(m_s, NEG_BIG)
    acc_s[...] = jnp.zeros_like(acc_s)

    def tile(ref, j):
        return ref[pl.ds(pl.multiple_of(j * tq, tq), tq), :]

    def scores(j, slot):
        s_s[slot] = _dot_nt(qq_s[...], tile(k_ref, j))

    def consume(j, slot, mask=None):
        s = s_s[slot]
        if mask is not None:
            s = jnp.where(mask, s, NEG_BIG)
        _softmax_update(s, tile(ve_s, j), m_s, acc_s)

    scores(0, 0)

    def pair(jj, c):
        j = 2 * jj
        scores(j + 1, 1)
        consume(j, 0)
        scores(j + 2, 0)
        consume(j + 1, 1)
        return c

    lax.fori_loop(0, i // 2, pair, 0)

    @pl.when(i % 2 == 1)
    def _():
        scores(i, 1)
        consume(i - 1, 0)
        s_s[0] = s_s[1]

    row = lax.broadcasted_iota(jnp.int32, s_s.shape[1:], 0)
    row = jnp.where(row >= tq, row - tq, row)
    col = lax.broadcasted_iota(jnp.int32, s_s.shape[1:], 1)
    consume(i, 0, mask=col <= row)

    o1 = acc_s[0:tq, 0:dv] / acc_s[0:tq, dv:]
    o2 = acc_s[tq:2 * tq, 0:dv] / acc_s[tq:2 * tq, dv:]
    o = o1 - _lam(lam_ref, lam_init) * o2
    o_ref[...] = _subln(o, gs_ref[...], lam_init).astype(BF16)


def _decode_attention(lam_ref, gs_ref, q_ref, kn_ref, vn_ref, kc_refs, vc_refs, o_ref, *, lam_init):
    n_heads = q_ref.shape[0]
    page, _, dk = kc_refs[0].shape
    dv = vc_refs[0].shape[2]

    qm = _split_components(q_ref[...])
    qmb = qm.astype(BF16)
    shape = (2 * n_heads, page * n_heads)
    row = lax.broadcasted_iota(jnp.int32, shape, 0)
    col = lax.broadcasted_iota(jnp.int32, shape, 1)
    same_head = ((row ^ col) & (n_heads - 1)) == 0

    kn2 = jnp.concatenate([kn_ref[...], kn_ref[...]], axis=0)
    vn2 = jnp.concatenate([vn_ref[...], vn_ref[...]], axis=0)
    s_self = jnp.sum(qm * kn2, axis=-1, keepdims=True)

    scores = []
    m = s_self
    for kc_ref in kc_refs:
        k2 = kc_ref[...].reshape(page * n_heads, dk).astype(BF16)
        s = jnp.where(same_head, _dot_nt(qmb, k2), NEG_BIG)
        scores.append(s)
        m = jnp.maximum(m, jnp.max(s, axis=-1, keepdims=True))

    p_self = jnp.exp2(s_self - m)
    l = p_self
    acc = p_self * vn2
    for s, vc_ref in zip(scores, vc_refs):
        p = jnp.exp2(s - m)
        l = l + jnp.sum(p, axis=-1, keepdims=True)
        acc = acc + _dot(p.astype(BF16), vc_ref[...].reshape(page * n_heads, dv).astype(BF16))

    o = acc / l
    o = o[0:n_heads, :] - _lam(lam_ref, lam_init) * o[n_heads:2 * n_heads, :]
    o_ref[...] = _subln(o, gs_ref[...], lam_init)


def _attn_kernel(pt_ref, lam_ref, gs_ref, q_ref, k_ref, v_ref, qd_ref, knd_ref, vnd_ref, ck_hbm, cv_hbm,
                 o_ref, od_ref, qq_s, ve_s, s_s, m_s, acc_s, kbuf, vbuf, sem, *, layer, tq, lam_init):
    n_pages = kbuf.shape[1]
    step = (pl.program_id(0) * pl.num_programs(1) + pl.program_id(1)) * pl.num_programs(2) + pl.program_id(2)
    n_steps = pl.num_programs(0) * pl.num_programs(1) * pl.num_programs(2)
    slot = lax.rem(step, 2)

    def page_copies(sample, slot):
        copies = []
        for p in range(n_pages):
            pg = pt_ref[sample, p]
            copies.append(pltpu.make_async_copy(ck_hbm.at[layer, pg], kbuf.at[slot, p], sem.at[slot, 0]))
            copies.append(pltpu.make_async_copy(cv_hbm.at[layer, pg], vbuf.at[slot, p], sem.at[slot, 1]))
        return copies

    @pl.when(step == 0)
    def _():
        for c in page_copies(0, 0):
            c.start()

    @pl.when(step + 1 < n_steps)
    def _():
        for c in page_copies(step + 1, 1 - slot):
            c.start()

    for c in page_copies(step, slot):
        c.wait()

    _decode_attention(lam_ref, gs_ref, qd_ref, knd_ref, vnd_ref,
                      [kbuf.at[slot, p] for p in range(n_pages)], [vbuf.at[slot, p] for p in range(n_pages)],
                      od_ref, lam_init=lam_init)
    _prompt_attention(lam_ref, gs_ref, q_ref, k_ref, v_ref, o_ref, qq_s, ve_s, s_s, m_s, acc_s,
                      tq=tq, lam_init=lam_init)


def _attention(qb, kb, vb, q3, kn3, vn3, cache_k, cache_v, page_table, lam4, gs, *, batch, seq, tq, layer, lam_init):
    m, dq = qb.shape
    n_heads = dq // V7X_LANES
    nq = seq // tq
    n_dec, _, dk = q3.shape
    n_pages = page_table.shape[1]
    page = cache_k.shape[2]
    dv = cache_v.shape[4]
    assert n_heads & (n_heads - 1) == 0 and dv == V7X_LANES
    assert n_dec == batch * n_heads * nq, "one decode sample per prompt-attention grid step"

    def sample(b, h, i):
        return (b * n_heads + h) * nq + i

    def sample_spec(d):
        return pl.BlockSpec((None, n_heads, d), lambda b, h, i, pt: (sample(b, h, i), 0, 0))

    tile_spec = pl.BlockSpec((tq, V7X_LANES), lambda b, h, i, pt: (b * nq + i, h))
    head_spec = pl.BlockSpec((seq, V7X_LANES), lambda b, h, i, pt: (b, h))
    hbm_spec = pl.BlockSpec(memory_space=pl.ANY)
    grid_spec = pltpu.PrefetchScalarGridSpec(
        num_scalar_prefetch=1,
        grid=(batch, n_heads, nq),
        in_specs=[
            pl.BlockSpec(lam4.shape, lambda b, h, i, pt: (0, 0)),
            pl.BlockSpec((1, dv), lambda b, h, i, pt: (0, 0)),
            tile_spec, head_spec, head_spec,
            sample_spec(dk), sample_spec(dk), sample_spec(dv),
            hbm_spec, hbm_spec,
        ],
        out_specs=[tile_spec, sample_spec(dv)],
        scratch_shapes=[pltpu.VMEM((2 * tq, V7X_LANES), BF16), pltpu.VMEM((seq, 2 * V7X_LANES), BF16),
                        pltpu.VMEM((2, 2 * tq, tq), F32), pltpu.VMEM((2 * tq, V7X_LANES), F32),
                        pltpu.VMEM((2 * tq, 2 * V7X_LANES), F32),
                        pltpu.VMEM((2, n_pages, page, n_heads, dk), F32),
                        pltpu.VMEM((2, n_pages, page, n_heads, dv), F32),
                        pltpu.SemaphoreType.DMA((2, 2))],
    )
    return pl.pallas_call(
        functools.partial(_attn_kernel, layer=layer, tq=tq, lam_init=lam_init),
        grid_spec=grid_spec,
        out_shape=[jax.ShapeDtypeStruct((m, dq), BF16), jax.ShapeDtypeStruct((n_dec, n_heads, dv), F32)],
        compiler_params=pltpu.CompilerParams(dimension_semantics=("arbitrary",) * 3,
                                             vmem_limit_bytes=ATTN_VMEM_LIMIT),
        name="attention",
    )(page_table, lam4, gs, qb, kb, vb, q3, kn3, vn3, cache_k, cache_v)


def _s5_discretize(are, aim, ldt):
    dt = jnp.exp(ldt)
    mag = jnp.exp(dt * are)
    lbr = mag * jnp.cos(dt * aim)
    lbi = mag * jnp.sin(dt * aim)
    nr = lbr - 1.0
    den = are * are + aim * aim
    cr = (nr * are + lbi * aim) / den
    ci = (lbi * are - nr * aim) / den
    return lbr, lbi, cr, ci


def _s5_input_weight(cr, ci, bre, bim):
    return jnp.concatenate([cr * bre - ci * bim, cr * bim + ci * bre], axis=1).astype(BF16)


def _s5_readout(hr, hi, cre_ref, cim_ref, d_ref, u):
    return (_dot_nt(hr.astype(BF16), cre_ref[...].astype(BF16))
            - _dot_nt(hi.astype(BF16), cim_ref[...].astype(BF16)) + d_ref[...] * u)


def _s5_prompt_kernel(are_ref, aim_ref, ldt_ref, bre_ref, bim_ref, cre_ref, cim_ref, d_ref, u_ref,
                      y_ref, hre_ref, him_ref, up_s, bur_s, bui_s, yp_s, wb_s, lam_s, car_s, *, seg):
    ch = pl.program_id(2)
    n_st = bur_s.shape[1]
    sub = V7X_SUBLANES

    @pl.when(ch == 0)
    def _():
        lbr, lbi, cr, ci = _s5_discretize(are_ref[...], aim_ref[...], ldt_ref[...])
        wb_s[...] = _s5_input_weight(cr, ci, bre_ref[...], bim_ref[...])
        lam_s[0:1, :] = lbr
        lam_s[1:2, :] = lbi
        pr, pi = lbr, lbi
        for _ in range(int(math.log2(seg))):
            pr, pi = pr * pr - pi * pi, 2.0 * pr * pi
        lam_s[2:3, :] = pr
        lam_s[3:4, :] = pi
        car_s[...] = jnp.zeros_like(car_s)

    def permute(t, c):
        up_s[pl.ds(pl.multiple_of(t * sub, sub), sub), :] = u_ref[pl.ds(t, sub, stride=seg), :]
        return c

    lax.fori_loop(0, seg, permute, 0, unroll=8)

    bu = _dot(up_s[...].astype(BF16), wb_s[...])
    bur_s[...] = bu[:, :n_st]
    bui_s[...] = bu[:, n_st:]

    lbr = jnp.broadcast_to(lam_s[0:1, :], (sub, n_st))
    lbi = jnp.broadcast_to(lam_s[1:2, :], (sub, n_st))

    def advance(t, c):
        sr, si = c
        r0 = pl.multiple_of(t * sub, sub)
        return (lbr * sr - lbi * si + bur_s[pl.ds(r0, sub), :],
                lbr * si + lbi * sr + bui_s[pl.ds(r0, sub), :])

    zero = jnp.zeros((sub, n_st), F32)
    er, ei = lax.fori_loop(0, seg, advance, (zero, zero), unroll=4)

    psr, psi = lam_s[2:3, :], lam_s[3:4, :]
    hr, hi = car_s[0:1, :], car_s[1:2, :]
    rows_r, rows_i = [], []
    for r in range(sub):
        rows_r.append(hr)
        rows_i.append(hi)
        hr, hi = (er[r:r + 1, :] + psr * hr - psi * hi, ei[r:r + 1, :] + psr * hi + psi * hr)
    car_s[0:1, :] = hr
    car_s[1:2, :] = hi

    def advance_store(t, c):
        sr, si = advance(t, c)
        r0 = pl.multiple_of(t * sub, sub)
        bur_s[pl.ds(r0, sub), :] = sr
        bui_s[pl.ds(r0, sub), :] = si
        return sr, si

    lax.fori_loop(0, seg, advance_store,
                  (jnp.concatenate(rows_r, axis=0), jnp.concatenate(rows_i, axis=0)), unroll=4)

    yp_s[...] = _s5_readout(bur_s[...], bui_s[...], cre_ref, cim_ref, d_ref, up_s[...])

    def unpermute(t, c):
        y_ref[pl.ds(t, sub, stride=seg), :] = yp_s[pl.ds(pl.multiple_of(t * sub, sub), sub), :]
        return c

    lax.fori_loop(0, seg, unpermute, 0, unroll=8)

    @pl.when(ch == pl.num_programs(2) - 1)
    def _():
        hre_ref[...] = hr
        him_ref[...] = hi


def _s5_prompt(u, lay, *, batch, seq, rows):
    m, d_ssm = u.shape
    are, aim, ldt, bre, bim, cre, cim, dsk = lay
    n_gb, n_ch_lanes, n_st = bre.shape
    n_ch = seq // rows
    seg = rows // V7X_SUBLANES
    lane_spec = pl.BlockSpec((None, 1, n_st), lambda b, g, c: (g, 0, 0))
    mat_spec = pl.BlockSpec((None, n_ch_lanes, n_st), lambda b, g, c: (g, 0, 0))
    row_spec = pl.BlockSpec((rows, n_ch_lanes), lambda b, g, c: (b * n_ch + c, g))
    st_spec = pl.BlockSpec((None, None, 1, n_st), lambda b, g, c: (b, g, 0, 0))
    st_shape = jax.ShapeDtypeStruct((batch, n_gb, 1, n_st), F32)
    return pl.pallas_call(
        functools.partial(_s5_prompt_kernel, seg=seg),
        grid=(batch, n_gb, n_ch),
        in_specs=[lane_spec, lane_spec, lane_spec, mat_spec, mat_spec, mat_spec, mat_spec,
                  pl.BlockSpec((1, n_ch_lanes), lambda b, g, c: (0, g)), row_spec],
        out_specs=[row_spec, st_spec, st_spec],
        out_shape=[jax.ShapeDtypeStruct((m, d_ssm), F32), st_shape, st_shape],
        scratch_shapes=[pltpu.VMEM((rows, n_ch_lanes), F32), pltpu.VMEM((rows, n_st), F32),
                        pltpu.VMEM((rows, n_st), F32), pltpu.VMEM((rows, n_ch_lanes), F32),
                        pltpu.VMEM((n_ch_lanes, 2 * n_st), BF16), pltpu.VMEM((4, n_st), F32),
                        pltpu.VMEM((2, n_st), F32)],
        compiler_params=_params(("parallel", "parallel", "arbitrary")),
        name="s5_prompt",
    )(are, aim, ldt, bre, bim, cre, cim, dsk, u)


def _s5_step_kernel(are_ref, aim_ref, ldt_ref, bre_ref, bim_ref, cre_ref, cim_ref, d_ref, u_ref,
                    h0r_ref, h0i_ref, y_ref, hre_ref, him_ref):
    n_st = h0r_ref.shape[1]
    lbr, lbi, cr, ci = _s5_discretize(are_ref[...], aim_ref[...], ldt_ref[...])
    u = u_ref[...]
    bu = _dot(u.astype(BF16), _s5_input_weight(cr, ci, bre_ref[...], bim_ref[...]))
    h0r, h0i = h0r_ref[...], h0i_ref[...]
    hr = lbr * h0r - lbi * h0i + bu[:, :n_st]
    hi = lbr * h0i + lbi * h0r + bu[:, n_st:]
    hre_ref[...] = hr
    him_ref[...] = hi
    y_ref[...] = _s5_readout(hr, hi, cre_ref, cim_ref, d_ref, u)


def _s5_step(u, h0r, h0i, lay):
    m, d_ssm = u.shape
    are, aim, ldt, bre, bim, cre, cim, dsk = lay
    n_gb, n_ch_lanes, n_st = bre.shape
    lane_spec = pl.BlockSpec((None, 1, n_st), lambda g: (g, 0, 0))
    mat_spec = pl.BlockSpec((None, n_ch_lanes, n_st), lambda g: (g, 0, 0))
    row_spec = pl.BlockSpec((m, n_ch_lanes), lambda g: (0, g))
    st_spec = pl.BlockSpec((m, n_st), lambda g: (0, g))
    st_shape = jax.ShapeDtypeStruct(h0r.shape, F32)
    return pl.pallas_call(
        _s5_step_kernel,
        grid=(n_gb,),
        in_specs=[lane_spec, lane_spec, lane_spec, mat_spec, mat_spec, mat_spec, mat_spec,
                  pl.BlockSpec((1, n_ch_lanes), lambda g: (0, g)), row_spec, st_spec, st_spec],
        out_specs=[row_spec, st_spec, st_spec],
        out_shape=[jax.ShapeDtypeStruct((m, d_ssm), F32), st_shape, st_shape],
        compiler_params=_params(("parallel",)),
        name="s5_step",
    )(are, aim, ldt, bre, bim, cre, cim, dsk, u, h0r, h0i)


def _s5_layout(a_re, a_im, log_dt, b_re, b_im, c_re, c_im, d_skip):
    n_g, n_st = a_re.shape
    gl = SSM_GROUP_BLOCK
    n_gb = n_g // gl
    same = jnp.eye(gl, dtype=jnp.bool_)[None, :, None, :, None]

    def lanes(a):
        return a.reshape(n_gb, 1, gl * n_st)

    def block_diag(w):
        w5 = w.reshape(n_gb, gl, SSM_GROUP, 1, n_st)
        return jnp.where(same, w5, 0.0).reshape(n_gb, gl * SSM_GROUP, gl * n_st)

    return (lanes(a_re), lanes(a_im), lanes(jnp.broadcast_to(log_dt[:, None], (n_g, n_st))),
            block_diag(b_re.transpose(0, 2, 1)), block_diag(b_im.transpose(0, 2, 1)),
            block_diag(c_re), block_diag(c_im), d_skip.reshape(1, n_g * SSM_GROUP))


def _mix_kernel(x_ref, ys_ref, o_ref, ga_ref, gb_ref, wglu_ref, wso_ref, wao_ref, wo_ref, out_ref):
    ys = ys_ref[...]
    ya = 0.5 * ys * (1.0 + lax.erf(ys * (2.0 ** -0.5)))
    glu = (ya * jax.nn.sigmoid(_dot(ya.astype(BF16), wglu_ref[...]))).astype(BF16)
    ssm = _dot(glu, wso_ref[...])
    att = _dot(o_ref[...], wao_ref[...])
    mix = (ga_ref[...].astype(F32) * ssm + gb_ref[...].astype(F32) * att).astype(BF16)
    out_ref[...] = x_ref[...] + _dot(mix, wo_ref[...])


def _mix(x, ys, o, gates, w_glu, w_ssm_out, w_attn_out, w_o, *, tm):
    m, d = x.shape
    d_ssm = ys.shape[1]
    d_att = o.shape[1]

    def resident(w):
        return pl.BlockSpec(w.shape, lambda i: (0, 0), pipeline_mode=pl.Buffered(1))

    return pl.pallas_call(
        _mix_kernel,
        grid=(m // tm,),
        in_specs=[
            pl.BlockSpec((tm, d), lambda i: (i, 0)),
            pl.BlockSpec((tm, d_ssm), lambda i: (i, 0)),
            pl.BlockSpec((tm, d_att), lambda i: (i, 0)),
            pl.BlockSpec((tm, d), lambda i: (i, 0)),
            pl.BlockSpec((tm, d), lambda i: (i, 1)),
            resident(w_glu), resident(w_ssm_out), resident(w_attn_out), resident(w_o),
        ],
        out_specs=pl.BlockSpec((tm, d), lambda i: (i, 0)),
        out_shape=jax.ShapeDtypeStruct((m, d), F32),
        compiler_params=_params(("parallel",)),
        name="mix",
    )(x, ys, o, gates, gates, w_glu, w_ssm_out, w_attn_out, w_o)


def _rotary_tables(pos, rows):
    half = ROT_DIM // 2
    inv = ROPE_THETA ** (-jnp.arange(half, dtype=F32) / half)
    ang = pos.astype(F32)[:, None] * inv[None, :]
    cos, sin = jnp.cos(ang), jnp.sin(ang)
    t = pos.shape[0]
    zh = jnp.zeros((t, half), F32)
    rest0 = jnp.zeros((t, HEAD_DIM - ROT_DIM), F32)
    cos_c = jnp.concatenate([cos, cos, jnp.ones((t, HEAD_DIM - ROT_DIM), F32)], axis=1)
    s1_c = jnp.concatenate([-sin, zh, rest0], axis=1)
    s2_c = jnp.concatenate([zh, sin, rest0], axis=1)
    return tuple(jnp.broadcast_to(jnp.tile(a, (1, 2)), (rows, 2 * HEAD_DIM)) for a in (cos_c, s1_c, s2_c))


def kernel(x_prompt, x_sample, cache_k, cache_v, state_ssm_re, state_ssm_im, page_table, norm_ffn1, ffn1_w1, ffn1_w3, ffn1_w2, norm_mix, w_in, lam_q1, lam_k1, lam_q2, lam_k2, g_subln, w_attn_out, ssm_a_re, ssm_a_im, ssm_log_dt, ssm_b_re, ssm_b_im, ssm_c_re, ssm_c_im, ssm_d, w_glu, w_ssm_out, w_o, norm_ffn2, ffn2_w1, ffn2_w3, ffn2_w2, norm_final):
    batch, seq, d = x_prompt.shape
    n_dec, dec_seq, _ = x_sample.shape
    assert dec_seq == 1
    depth = cache_k.shape[0]
    n_heads, d_k = cache_k.shape[3], cache_k.shape[4]
    d_v = cache_v.shape[4]
    n_groups, n_state = ssm_a_re.shape[1], ssm_a_re.shape[2]
    d_qk, d_att, d_ssm = n_heads * d_k, n_heads * d_v, n_groups * SSM_GROUP
    past_len = page_table.shape[1] * cache_k.shape[2]
    m_p = batch * seq

    tabs_p = _rotary_tables(jnp.arange(seq, dtype=jnp.int32), seq)
    tabs_s = _rotary_tables(past_len + jnp.arange(dec_seq, dtype=jnp.int32), n_dec)
    row = lambda a: a.reshape(1, -1)
    gfin = row(norm_final)

    h_p = x_prompt.reshape(m_p, d)
    h_s = x_sample.reshape(n_dec, d)
    outs = [[] for _ in range(8)]
    for l in range(depth):
        lam_init = 0.8 - 0.6 * math.exp(-0.3 * l)
        last = l == depth - 1
        win = w_in[l].astype(BF16)
        wglu, wso = w_glu[l].astype(BF16), w_ssm_out[l].astype(BF16)
        wao, wo = w_attn_out[l].astype(BF16), w_o[l].astype(BF16)
        lam4 = jnp.stack([lam_q1[l], lam_k1[l], lam_q2[l], lam_k2[l]])
        gs = row(g_subln[l])
        lay = _s5_layout(ssm_a_re[l], ssm_a_im[l], ssm_log_dt[l], ssm_b_re[l], ssm_b_im[l],
                         ssm_c_re[l], ssm_c_im[l], ssm_d[l])

        def half1(x, tm, w):
            return _ffn(x, row(norm_ffn1[l]), *w, row(norm_mix[l]), norm_out="extra", tm=tm, tf=512)

        def half2(x, tm, w):
            return _ffn(x, row(norm_ffn2[l]), *w, gfin, norm_out="replace" if last else "none", tm=tm, tf=512)

        x1_s, xn_s, *w1_bf = half1(h_s, n_dec, (ffn1_w1[l], ffn1_w3[l], ffn1_w2[l]))
        q_s, k_s, _, v_s, _, u_s, gates_s = _project(xn_s, win, *tabs_s, tm=n_dec,
                                                     d_qk=d_qk, d_v=d_att, d_ssm=d_ssm)
        x1_p, xn_p = half1(h_p, 512, w1_bf)
        q, k, kb, v, vb, u, gates = _project(xn_p, win, *tabs_p, tm=1024,
                                             d_qk=d_qk, d_v=d_att, d_ssm=d_ssm)

        o, o_s = _attention(q, kb, vb, q_s.astype(F32).reshape(n_dec, n_heads, d_k),
                            k_s.reshape(n_dec, n_heads, d_k), v_s.reshape(n_dec, n_heads, d_v),
                            cache_k, cache_v, page_table, lam4, gs,
                            batch=batch, seq=seq, tq=512, layer=l, lam_init=lam_init)

        ys, hre, him = _s5_step(u_s, state_ssm_re[l].reshape(n_dec, n_groups * n_state),
                                state_ssm_im[l].reshape(n_dec, n_groups * n_state), lay)
        x2 = _mix(x1_s, ys, o_s.reshape(n_dec, d_att).astype(BF16), gates_s, wglu, wso, wao, wo, tm=n_dec)
        h_s, *w2_bf = half2(x2, n_dec, (ffn2_w1[l], ffn2_w3[l], ffn2_w2[l]))
        outs[4].append(k_s.reshape(n_dec, dec_seq, n_heads, d_k))
        outs[5].append(v_s.reshape(n_dec, dec_seq, n_heads, d_v))
        outs[6].append(hre.reshape(n_dec, n_groups, n_state))
        outs[7].append(him.reshape(n_dec, n_groups, n_state))

        ys, hre, him = _s5_prompt(u, lay, batch=batch, seq=seq, rows=1024)
        x2 = _mix(x1_p, ys, o, gates, wglu, wso, wao, wo, tm=256)
        (h_p,) = half2(x2, 512, w2_bf)
        outs[0].append(k.reshape(batch, seq, n_heads, d_k))
        outs[1].append(v.reshape(batch, seq, n_heads, d_v))
        outs[2].append(hre.reshape(batch, n_groups, n_state))
        outs[3].append(him.reshape(batch, n_groups, n_state))

    return (h_p.reshape(batch, seq, d), h_s.reshape(n_dec, dec_seq, d)) + tuple(jnp.stack(o) for o in outs)
```

```python
import functools
import math

import jax
import jax.numpy as jnp
from jax import lax
from jax.experimental import pallas as pl
from jax.experimental.pallas import tpu as pltpu

F32 = jnp.float32
BF16 = jnp.bfloat16

HEAD_DIM = 64
ROT_DIM = HEAD_DIM // 4
ROPE_THETA = 500000.0
EPS = 1e-6
SSM_GROUP = 16
SSM_GROUP_BLOCK = 8

V7X_LANES = 128
V7X_SUBLANES = 8
V7X_MXU_WIDTH = 256
MXU_WIDTH = V7X_MXU_WIDTH
V7X_VMEM_BYTES = 64 * 1024 * 1024
VMEM_LIMIT = V7X_VMEM_BYTES * 3 // 4
ATTN_VMEM_LIMIT = V7X_VMEM_BYTES * 7 // 8

NEG_BIG = -1e30
Q_SCALE = HEAD_DIM ** -0.5 * math.log2(math.e)


def _dot(a, b):
    return jnp.dot(a, b, preferred_element_type=F32)


def _dot_nt(a, b):
    return lax.dot_general(a, b, (((1,), (1,)), ((), ())), preferred_element_type=F32)


def _rms(x, g):
    return x * lax.rsqrt(jnp.mean(x * x, axis=-1, keepdims=True) + EPS) * g


def _params(sem):
    return pltpu.CompilerParams(dimension_semantics=sem, vmem_limit_bytes=VMEM_LIMIT)


def _ffn_kernel(x_ref, g_ref, w1_ref, w3_ref, w2_ref, g2_ref, o_ref, *refs, norm_out, emit_bf16):
    f = pl.program_id(1)
    refs = list(refs)
    xn2_ref = refs.pop(0) if norm_out == "extra" else None
    if emit_bf16:
        w1b_ref, w3b_ref, w2b_ref, xn_s, acc_s = refs
        w1b_ref[...] = w1_ref[...].astype(BF16)
        w3b_ref[...] = w3_ref[...].astype(BF16)
        w2b_ref[...] = w2_ref[...].astype(BF16)
        w1_ref, w3_ref, w2_ref = w1b_ref, w3b_ref, w2b_ref
    else:
        xn_s, acc_s = refs

    @pl.when(f == 0)
    def _():
        xn_s[...] = _rms(x_ref[...], g_ref[...]).astype(BF16)
        acc_s[...] = jnp.zeros_like(acc_s)

    xn = xn_s[...]
    acc = acc_s[...]
    for c in range(w1_ref.shape[1] // MXU_WIDTH):
        cols = slice(c * MXU_WIDTH, (c + 1) * MXU_WIDTH)
        h1 = _dot(xn, w1_ref[:, cols])
        h3 = _dot(xn, w3_ref[:, cols])
        hh = (h1 * jax.nn.sigmoid(h1) * h3).astype(BF16)
        acc = acc + _dot(hh, w2_ref[cols, :])
    acc_s[...] = acc

    @pl.when(f == pl.num_programs(1) - 1)
    def _():
        y = x_ref[...] + 0.5 * acc_s[...]
        if norm_out == "replace":
            y = _rms(y, g2_ref[...])
        elif norm_out == "extra":
            xn2_ref[...] = _rms(y, g2_ref[...]).astype(BF16)
        o_ref[...] = y


def _ffn(x, g, w1, w3, w2, g2, *, norm_out, tm, tf):
    m, d = x.shape
    dff = w1.shape[1]
    emit_bf16 = w1.dtype == F32
    assert not emit_bf16 or m == tm, "each weight tile must be visited exactly once"
    up_spec = pl.BlockSpec((d, tf), lambda i, f: (0, f))
    down_spec = pl.BlockSpec((tf, d), lambda i, f: (f, 0))
    row_spec = pl.BlockSpec((tm, d), lambda i, f: (i, 0))
    gain_spec = pl.BlockSpec((1, d), lambda i, f: (0, 0))
    out_specs = [row_spec]
    out_shape = [jax.ShapeDtypeStruct((m, d), F32)]
    if norm_out == "extra":
        out_specs += [row_spec]
        out_shape += [jax.ShapeDtypeStruct((m, d), BF16)]
    if emit_bf16:
        out_specs += [up_spec, up_spec, down_spec]
        out_shape += [jax.ShapeDtypeStruct(w.shape, BF16) for w in (w1, w3, w2)]
    return pl.pallas_call(
        functools.partial(_ffn_kernel, norm_out=norm_out, emit_bf16=emit_bf16),
        grid=(m // tm, dff // tf),
        in_specs=[row_spec, gain_spec, up_spec, up_spec, down_spec, gain_spec],
        out_specs=out_specs,
        out_shape=out_shape,
        scratch_shapes=[pltpu.VMEM((tm, d), BF16), pltpu.VMEM((tm, d), F32)],
        compiler_params=_params(("parallel", "arbitrary")),
        name="ffn",
    )(x, g, w1, w3, w2, g2)


def _rotate(a, cos_ref, s1_ref, s2_ref):
    cos, s1, s2 = cos_ref[...], s1_ref[...], s2_ref[...]
    half = ROT_DIM // 2
    out = []
    for h in range(a.shape[1] // V7X_LANES):
        blk = a[:, h * V7X_LANES:(h + 1) * V7X_LANES]
        up = pltpu.roll(blk, V7X_LANES - half, 1)
        dn = pltpu.roll(blk, half, 1)
        out.append(blk * cos + up * s1 + dn * s2)
    return jnp.concatenate(out, axis=1)


def _col_chunks(n):
    return [slice(c, c + MXU_WIDTH) for c in range(0, n, MXU_WIDTH)]


def _proj_qk_kernel(xn_ref, w_ref, cos_ref, s1_ref, s2_ref, q_ref, k_ref, kb_ref):
    n = q_ref.shape[1]
    for cols in _col_chunks(n):
        r = _rotate(_dot(xn_ref[...], w_ref[:, cols]), cos_ref, s1_ref, s2_ref)
        q_ref[:, cols] = (r * Q_SCALE).astype(BF16)
    for cols in _col_chunks(n):
        r = _rotate(_dot(xn_ref[...], w_ref[:, slice(n + cols.start, n + cols.stop)]), cos_ref, s1_ref, s2_ref)
        k_ref[:, cols] = r
        kb_ref[:, cols] = r.astype(BF16)


def _proj_vu_kernel(xn_ref, w_ref, v_ref, vb_ref, u_ref):
    n = v_ref.shape[1]
    for cols in _col_chunks(n):
        a = _dot(xn_ref[...], w_ref[:, cols])
        v_ref[:, cols] = a
        vb_ref[:, cols] = a.astype(BF16)
    for cols in _col_chunks(u_ref.shape[1]):
        u_ref[:, cols] = _dot(xn_ref[...], w_ref[:, slice(n + cols.start, n + cols.stop)])


def _proj_gate_kernel(xn_ref, w_ref, s_ref):
    for cols in _col_chunks(s_ref.shape[1]):
        a = _dot(xn_ref[...], w_ref[:, cols])
        s_ref[:, cols] = (0.5 * jnp.tanh(0.5 * a) + 0.5).astype(BF16)


def _project(xn, w_in, cos_t, s1_t, s2_t, *, tm, d_qk, d_v, d_ssm):
    m, d = xn.shape
    tn = 2 * d_qk
    assert d_v + d_ssm == tn and (w_in.shape[1] - 2 * tn) % tn == 0
    n_gate = (w_in.shape[1] - 2 * tn) // tn
    n_t = cos_t.shape[0] // tm
    x_spec = pl.BlockSpec((tm, d), lambda i, j: (i, 0))
    tab_spec = pl.BlockSpec((tm, V7X_LANES), lambda i, j: (i % n_t, 0))

    def row_spec(n):
        return pl.BlockSpec((tm, n), lambda i, j: (i, 0))

    def slab_spec(first):
        return pl.BlockSpec((d, tn), lambda i, j: (0, first + j))

    q, k, kb = pl.pallas_call(
        _proj_qk_kernel,
        grid=(m // tm, 1),
        in_specs=[x_spec, slab_spec(0), tab_spec, tab_spec, tab_spec],
        out_specs=[row_spec(d_qk)] * 3,
        out_shape=[jax.ShapeDtypeStruct((m, d_qk), BF16), jax.ShapeDtypeStruct((m, d_qk), F32),
                   jax.ShapeDtypeStruct((m, d_qk), BF16)],
        compiler_params=_params(("parallel", "arbitrary")),
        name="proj_qk",
    )(xn, w_in, cos_t, s1_t, s2_t)

    v, vb, u = pl.pallas_call(
        _proj_vu_kernel,
        grid=(m // tm, 1),
        in_specs=[x_spec, slab_spec(1)],
        out_specs=[row_spec(d_v), row_spec(d_v), row_spec(d_ssm)],
        out_shape=[jax.ShapeDtypeStruct((m, d_v), F32), jax.ShapeDtypeStruct((m, d_v), BF16),
                   jax.ShapeDtypeStruct((m, d_ssm), F32)],
        compiler_params=_params(("parallel", "arbitrary")),
        name="proj_vu",
    )(xn, w_in)

    gates = pl.pallas_call(
        _proj_gate_kernel,
        grid=(m // tm, n_gate),
        in_specs=[x_spec, slab_spec(2)],
        out_specs=pl.BlockSpec((tm, tn), lambda i, j: (i, j)),
        out_shape=jax.ShapeDtypeStruct((m, n_gate * tn), BF16),
        compiler_params=_params(("parallel", "arbitrary")),
        name="proj_gate",
    )(xn, w_in)
    return q, k, kb, v, vb, u, gates


def _lam(lam_ref, lam_init):
    a = jnp.sum(lam_ref[0:1, :] * lam_ref[1:2, :], axis=-1, keepdims=True)
    b = jnp.sum(lam_ref[2:3, :] * lam_ref[3:4, :], axis=-1, keepdims=True)
    return jnp.exp(a) - jnp.exp(b) + lam_init


def _subln(o, gs, lam_init):
    return _rms(o, gs) * (1.0 - lam_init)


def _split_components(q):
    lane = lax.broadcasted_iota(jnp.int32, q.shape, 1)
    zero = jnp.zeros_like(q)
    return jnp.concatenate([jnp.where(lane < HEAD_DIM, q, zero), jnp.where(lane >= HEAD_DIM, q, zero)], axis=0)


def _softmax_update(s, v_ext, m_s, acc_s):
    m_prev = m_s[...]
    m_new = jnp.maximum(m_prev, jnp.max(s, axis=-1, keepdims=True))
    alpha = jnp.exp2(m_prev - m_new)
    p = jnp.exp2((s - jnp.tile(m_new, (1, s.shape[1] // V7X_LANES))).astype(BF16))
    acc_s[...] = jnp.tile(alpha, (1, 2)) * acc_s[...] + _dot(p, v_ext)
    m_s[...] = m_new


def _prompt_attention(lam_ref, gs_ref, q_ref, k_ref, v_ref, o_ref, qq_s, ve_s, s_s, m_s, acc_s, *, tq, lam_init):
    i = pl.program_id(2)
    dv = v_ref.shape[1]

    @pl.when(i == 0)
    def _():
        ve_s[:, 0:dv] = v_ref[...]
        ve_s[:, dv:] = jnp.ones((ve_s.shape[0], ve_s.shape[1] - dv), BF16)

    qq_s[...] = _split_components(q_ref[...])
    m_s[...] = jnp.full_like(m_s, NEG_BIG)
    acc_s[...] = jnp.zeros_like(acc_s)

    def tile(ref, j):
        return ref[pl.ds(pl.multiple_of(j * tq, tq), tq), :]

    def scores(j, slot):
        s_s[slot] = _dot_nt(qq_s[...], tile(k_ref, j))

    def consume(j, slot, mask=None):
        s = s_s[slot]
        if mask is not None:
            s = jnp.where(mask, s, NEG_BIG)
        _softmax_update(s, tile(ve_s, j), m_s, acc_s)

    scores(0, 0)

    def pair(jj, c):
        j = 2 * jj
        scores(j + 1, 1)
        consume(j, 0)
        scores(j + 2, 0)
        consume(j + 1, 1)
        return c

    lax.fori_loop(0, i // 2, pair, 0)

    row = lax.broadcasted_iota(jnp.int32, s_s.shape[1:], 0)
    row = jnp.where(row >= tq, row - tq, row)
    col = lax.broadcasted_iota(jnp.int32, s_s.shape[1:], 1)
    causal = col <= row

    @pl.when(i % 2 == 1)
    def _():
        scores(i, 1)
        consume(i - 1, 0)
        consume(i, 1, mask=causal)

    @pl.when(i % 2 == 0)
    def _():
        consume(i, 0, mask=causal)

    o1 = acc_s[0:tq, 0:dv] / acc_s[0:tq, dv:]
    o2 = acc_s[tq:2 * tq, 0:dv] / acc_s[tq:2 * tq, dv:]
    o = o1 - _lam(lam_ref, lam_init) * o2
    o_ref[...] = _subln(o, gs_ref[...], lam_init).astype(BF16)


def _decode_attention(lam_ref, gs_ref, q_ref, kn_ref, vn_ref, kc_refs, vc_refs, o_ref, *, lam_init):
    n_heads = q_ref.shape[0]
    page, _, dk = kc_refs[0].shape
    dv = vc_refs[0].shape[2]

    qm = _split_components(q_ref[...])
    qmb = qm.astype(BF16)
    shape = (2 * n_heads, page * n_heads)
    row = lax.broadcasted_iota(jnp.int32, shape, 0)
    col = lax.broadcasted_iota(jnp.int32, shape, 1)
    same_head = ((row ^ col) & (n_heads - 1)) == 0

    kn2 = jnp.concatenate([kn_ref[...], kn_ref[...]], axis=0)
    vn2 = jnp.concatenate([vn_ref[...], vn_ref[...]], axis=0)
    s_self = jnp.sum(qm * kn2, axis=-1, keepdims=True)

    scores = []
    m = s_self
    for kc_ref in kc_refs:
        k2 = kc_ref[...].reshape(page * n_heads, dk).astype(BF16)
        s = jnp.where(same_head, _dot_nt(qmb, k2), NEG_BIG)
        scores.append(s)
        m = jnp.maximum(m, jnp.max(s, axis=-1, keepdims=True))

    p_self = jnp.exp2(s_self - m)
    l = p_self
    acc = p_self * vn2
    for s, vc_ref in zip(scores, vc_refs):
        p = jnp.exp2(s - m)
        l = l + jnp.sum(p, axis=-1, keepdims=True)
        acc = acc + _dot(p.astype(BF16), vc_ref[...].reshape(page * n_heads, dv).astype(BF16))

    o = acc / l
    o = o[0:n_heads, :] - _lam(lam_ref, lam_init) * o[n_heads:2 * n_heads, :]
    o_ref[...] = _subln(o, gs_ref[...], lam_init)


def _attn_kernel(pt_ref, lam_ref, gs_ref, q_ref, k_ref, v_ref, qd_ref, knd_ref, vnd_ref, ck_hbm, cv_hbm,
                 o_ref, od_ref, qq_s, ve_s, s_s, m_s, acc_s, kbuf, vbuf, sem, *, layer, tq, lam_init):
    n_pages = kbuf.shape[1]
    step = (pl.program_id(0) * pl.num_programs(1) + pl.program_id(1)) * pl.num_programs(2) + pl.program_id(2)
    n_steps = pl.num_programs(0) * pl.num_programs(1) * pl.num_programs(2)
    slot = lax.rem(step, 2)

    def page_copies(sample, slot):
        copies = []
        for p in range(n_pages):
            pg = pt_ref[sample, p]
            copies.append(pltpu.make_async_copy(ck_hbm.at[layer, pg], kbuf.at[slot, p], sem.at[slot, 0]))
            copies.append(pltpu.make_async_copy(cv_hbm.at[layer, pg], vbuf.at[slot, p], sem.at[slot, 1]))
        return copies

    @pl.when(step == 0)
    def _():
        for c in page_copies(0, 0):
            c.start()

    @pl.when(step + 1 < n_steps)
    def _():
        for c in page_copies(step + 1, 1 - slot):
            c.start()

    for c in page_copies(step, slot):
        c.wait()

    _decode_attention(lam_ref, gs_ref, qd_ref, knd_ref, vnd_ref,
                      [kbuf.at[slot, p] for p in range(n_pages)], [vbuf.at[slot, p] for p in range(n_pages)],
                      od_ref, lam_init=lam_init)
    _prompt_attention(lam_ref, gs_ref, q_ref, k_ref, v_ref, o_ref, qq_s, ve_s, s_s, m_s, acc_s,
                      tq=tq, lam_init=lam_init)


def _attention(qb, kb, vb, q3, kn3, vn3, cache_k, cache_v, page_table, lam4, gs, *, batch, seq, tq, layer, lam_init):
    m, dq = qb.shape
    n_heads = dq // V7X_LANES
    nq = seq // tq
    n_dec, _, dk = q3.shape
    n_pages = page_table.shape[1]
    page = cache_k.shape[2]
    dv = cache_v.shape[4]
    assert n_heads & (n_heads - 1) == 0 and dv == V7X_LANES
    assert n_dec == batch * n_heads * nq, "one decode sample per prompt-attention grid step"

    def sample(b, h, i):
        return (b * n_heads + h) * nq + i

    def sample_spec(d):
        return pl.BlockSpec((None, n_heads, d), lambda b, h, i, pt: (sample(b, h, i), 0, 0))

    tile_spec = pl.BlockSpec((tq, V7X_LANES), lambda b, h, i, pt: (b * nq + i, h))
    head_spec = pl.BlockSpec((seq, V7X_LANES), lambda b, h, i, pt: (b, h))
    hbm_spec = pl.BlockSpec(memory_space=pl.ANY)
    grid_spec = pltpu.PrefetchScalarGridSpec(
        num_scalar_prefetch=1,
        grid=(batch, n_heads, nq),
        in_specs=[
            pl.BlockSpec(lam4.shape, lambda b, h, i, pt: (0, 0)),
            pl.BlockSpec((1, dv), lambda b, h, i, pt: (0, 0)),
            tile_spec, head_spec, head_spec,
            sample_spec(dk), sample_spec(dk), sample_spec(dv),
            hbm_spec, hbm_spec,
        ],
        out_specs=[tile_spec, sample_spec(dv)],
        scratch_shapes=[pltpu.VMEM((2 * tq, V7X_LANES), BF16), pltpu.VMEM((seq, 2 * V7X_LANES), BF16),
                        pltpu.VMEM((2, 2 * tq, tq), F32), pltpu.VMEM((2 * tq, V7X_LANES), F32),
                        pltpu.VMEM((2 * tq, 2 * V7X_LANES), F32),
                        pltpu.VMEM((2, n_pages, page, n_heads, dk), F32),
                        pltpu.VMEM((2, n_pages, page, n_heads, dv), F32),
                        pltpu.SemaphoreType.DMA((2, 2))],
    )
    return pl.pallas_call(
        functools.partial(_attn_kernel, layer=layer, tq=tq, lam_init=lam_init),
        grid_spec=grid_spec,
        out_shape=[jax.ShapeDtypeStruct((m, dq), BF16), jax.ShapeDtypeStruct((n_dec, n_heads, dv), F32)],
        compiler_params=pltpu.CompilerParams(dimension_semantics=("arbitrary",) * 3,
                                             vmem_limit_bytes=ATTN_VMEM_LIMIT),
        name="attention",
    )(page_table, lam4, gs, qb, kb, vb, q3, kn3, vn3, cache_k, cache_v)


def _s5_discretize(are, aim, ldt):
    dt = jnp.exp(ldt)
    mag = jnp.exp(dt * are)
    lbr = mag * jnp.cos(dt * aim)
    lbi = mag * jnp.sin(dt * aim)
    nr = lbr - 1.0
    den = are * are + aim * aim
    cr = (nr * are + lbi * aim) / den
    ci = (lbi * are - nr * aim) / den
    return lbr, lbi, cr, ci


def _s5_input_weight(cr, ci, bre, bim):
    return jnp.concatenate([cr * bre - ci * bim, cr * bim + ci * bre], axis=1).astype(BF16)


def _s5_readout(hr, hi, cre_ref, cim_ref, d_ref, u):
    return (_dot_nt(hr.astype(BF16), cre_ref[...].astype(BF16))
            - _dot_nt(hi.astype(BF16), cim_ref[...].astype(BF16)) + d_ref[...] * u)


def _s5_prompt_kernel(are_ref, aim_ref, ldt_ref, bre_ref, bim_ref, cre_ref, cim_ref, d_ref, u_ref,
                      y_ref, hre_ref, him_ref, up_s, bur_s, bui_s, yp_s, pad_s, wb_s, lam_s, car_s, *, seg):
    ch = pl.program_id(2)
    n_st = bur_s.shape[1]
    sub = V7X_SUBLANES

    @pl.when(ch == 0)
    def _():
        lbr, lbi, cr, ci = _s5_discretize(are_ref[...], aim_ref[...], ldt_ref[...])
        wb_s[...] = _s5_input_weight(cr, ci, bre_ref[...], bim_ref[...])
        lam_s[0:1, :] = lbr
        lam_s[1:2, :] = lbi
        pr, pi = lbr, lbi
        for _ in range(int(math.log2(seg))):
            pr, pi = pr * pr - pi * pi, 2.0 * pr * pi
        lam_s[2:3, :] = pr
        lam_s[3:4, :] = pi
        car_s[...] = jnp.zeros_like(car_s)

    pitch = seg + sub
    for r in range(sub):
        pad_s[r * pitch:r * pitch + seg, :] = u_ref[r * seg:(r + 1) * seg, :]

    def permute(t, c):
        up_s[pl.ds(pl.multiple_of(t * sub, sub), sub), :] = pad_s[pl.ds(t, sub, stride=pitch), :]
        return c

    lax.fori_loop(0, seg, permute, 0, unroll=8)

    bu = _dot(up_s[...].astype(BF16), wb_s[...])
    bur_s[...] = bu[:, :n_st]
    bui_s[...] = bu[:, n_st:]

    lbr = jnp.broadcast_to(lam_s[0:1, :], (sub, n_st))
    lbi = jnp.broadcast_to(lam_s[1:2, :], (sub, n_st))

    def advance(t, c):
        sr, si = c
        r0 = pl.multiple_of(t * sub, sub)
        return (lbr * sr - lbi * si + bur_s[pl.ds(r0, sub), :],
                lbr * si + lbi * sr + bui_s[pl.ds(r0, sub), :])

    zero = jnp.zeros((sub, n_st), F32)
    er, ei = lax.fori_loop(0, seg, advance, (zero, zero), unroll=4)

    psr, psi = lam_s[2:3, :], lam_s[3:4, :]
    hr, hi = car_s[0:1, :], car_s[1:2, :]
    rows_r, rows_i = [], []
    for r in range(sub):
        rows_r.append(hr)
        rows_i.append(hi)
        hr, hi = (er[r:r + 1, :] + psr * hr - psi * hi, ei[r:r + 1, :] + psr * hi + psi * hr)
    car_s[0:1, :] = hr
    car_s[1:2, :] = hi

    def advance_store(t, c):
        sr, si = advance(t, c)
        r0 = pl.multiple_of(t * sub, sub)
        bur_s[pl.ds(r0, sub), :] = sr
        bui_s[pl.ds(r0, sub), :] = si
        return sr, si

    lax.fori_loop(0, seg, advance_store,
                  (jnp.concatenate(rows_r, axis=0), jnp.concatenate(rows_i, axis=0)), unroll=4)

    yp_s[...] = _s5_readout(bur_s[...], bui_s[...], cre_ref, cim_ref, d_ref, up_s[...])

    def unpermute(t, c):
        pad_s[pl.ds(t, sub, stride=pitch), :] = yp_s[pl.ds(pl.multiple_of(t * sub, sub), sub), :]
        return c

    lax.fori_loop(0, seg, unpermute, 0, unroll=8)
    for r in range(sub):
        y_ref[r * seg:(r + 1) * seg, :] = pad_s[r * pitch:r * pitch + seg, :]

    @pl.when(ch == pl.num_programs(2) - 1)
    def _():
        hre_ref[...] = hr
        him_ref[...] = hi


def _s5_prompt(u, lay, *, batch, seq, rows):
    m, d_ssm = u.shape
    are, aim, ldt, bre, bim, cre, cim, dsk = lay
    n_gb, n_ch_lanes, n_st = bre.shape
    n_ch = seq // rows
    seg = rows // V7X_SUBLANES
    lane_spec = pl.BlockSpec((None, 1, n_st), lambda b, g, c: (g, 0, 0))
    mat_spec = pl.BlockSpec((None, n_ch_lanes, n_st), lambda b, g, c: (g, 0, 0))
    row_spec = pl.BlockSpec((rows, n_ch_lanes), lambda b, g, c: (b * n_ch + c, g))
    st_spec = pl.BlockSpec((None, None, 1, n_st), lambda b, g, c: (b, g, 0, 0))
    st_shape = jax.ShapeDtypeStruct((batch, n_gb, 1, n_st), F32)
    return pl.pallas_call(
        functools.partial(_s5_prompt_kernel, seg=seg),
        grid=(batch, n_gb, n_ch),
        in_specs=[lane_spec, lane_spec, lane_spec, mat_spec, mat_spec, mat_spec, mat_spec,
                  pl.BlockSpec((1, n_ch_lanes), lambda b, g, c: (0, g)), row_spec],
        out_specs=[row_spec, st_spec, st_spec],
        out_shape=[jax.ShapeDtypeStruct((m, d_ssm), F32), st_shape, st_shape],
        scratch_shapes=[pltpu.VMEM((rows, n_ch_lanes), F32), pltpu.VMEM((rows, n_st), F32),
                        pltpu.VMEM((rows, n_st), F32), pltpu.VMEM((rows, n_ch_lanes), F32),
                        pltpu.VMEM((rows + V7X_SUBLANES * V7X_SUBLANES, n_ch_lanes), F32),
                        pltpu.VMEM((n_ch_lanes, 2 * n_st), BF16), pltpu.VMEM((4, n_st), F32),
                        pltpu.VMEM((2, n_st), F32)],
        compiler_params=_params(("parallel", "parallel", "arbitrary")),
        name="s5_prompt",
    )(are, aim, ldt, bre, bim, cre, cim, dsk, u)


def _s5_step_kernel(are_ref, aim_ref, ldt_ref, bre_ref, bim_ref, cre_ref, cim_ref, d_ref, u_ref,
                    h0r_ref, h0i_ref, y_ref, hre_ref, him_ref):
    n_st = h0r_ref.shape[1]
    lbr, lbi, cr, ci = _s5_discretize(are_ref[...], aim_ref[...], ldt_ref[...])
    u = u_ref[...]
    bu = _dot(u.astype(BF16), _s5_input_weight(cr, ci, bre_ref[...], bim_ref[...]))
    h0r, h0i = h0r_ref[...], h0i_ref[...]
    hr = lbr * h0r - lbi * h0i + bu[:, :n_st]
    hi = lbr * h0i + lbi * h0r + bu[:, n_st:]
    hre_ref[...] = hr
    him_ref[...] = hi
    y_ref[...] = _s5_readout(hr, hi, cre_ref, cim_ref, d_ref, u)


def _s5_step(u, h0r, h0i, lay):
    m, d_ssm = u.shape
    are, aim, ldt, bre, bim, cre, cim, dsk = lay
    n_gb, n_ch_lanes, n_st = bre.shape
    lane_spec = pl.BlockSpec((None, 1, n_st), lambda g: (g, 0, 0))
    mat_spec = pl.BlockSpec((None, n_ch_lanes, n_st), lambda g: (g, 0, 0))
    row_spec = pl.BlockSpec((m, n_ch_lanes), lambda g: (0, g))
    st_spec = pl.BlockSpec((m, n_st), lambda g: (0, g))
    st_shape = jax.ShapeDtypeStruct(h0r.shape, F32)
    return pl.pallas_call(
        _s5_step_kernel,
        grid=(n_gb,),
        in_specs=[lane_spec, lane_spec, lane_spec, mat_spec, mat_spec, mat_spec, mat_spec,
                  pl.BlockSpec((1, n_ch_lanes), lambda g: (0, g)), row_spec, st_spec, st_spec],
        out_specs=[row_spec, st_spec, st_spec],
        out_shape=[jax.ShapeDtypeStruct((m, d_ssm), F32), st_shape, st_shape],
        compiler_params=_params(("parallel",)),
        name="s5_step",
    )(are, aim, ldt, bre, bim, cre, cim, dsk, u, h0r, h0i)


def _s5_layout(a_re, a_im, log_dt, b_re, b_im, c_re, c_im, d_skip):
    n_g, n_st = a_re.shape
    gl = SSM_GROUP_BLOCK
    n_gb = n_g // gl
    same = jnp.eye(gl, dtype=jnp.bool_)[None, :, None, :, None]

    def lanes(a):
        return a.reshape(n_gb, 1, gl * n_st)

    def block_diag(w):
        w5 = w.reshape(n_gb, gl, SSM_GROUP, 1, n_st)
        return jnp.where(same, w5, 0.0).reshape(n_gb, gl * SSM_GROUP, gl * n_st)

    return (lanes(a_re), lanes(a_im), lanes(jnp.broadcast_to(log_dt[:, None], (n_g, n_st))),
            block_diag(b_re.transpose(0, 2, 1)), block_diag(b_im.transpose(0, 2, 1)),
            block_diag(c_re), block_diag(c_im), d_skip.reshape(1, n_g * SSM_GROUP))


def _mix_kernel(x_ref, ys_ref, o_ref, ga_ref, gb_ref, wglu_ref, wso_ref, wao_ref, wo_ref, out_ref):
    ys = ys_ref[...]
    ya = 0.5 * ys * (1.0 + lax.erf(ys * (2.0 ** -0.5)))
    glu = (ya * jax.nn.sigmoid(_dot(ya.astype(BF16), wglu_ref[...]))).astype(BF16)
    ssm = _dot(glu, wso_ref[...])
    att = _dot(o_ref[...], wao_ref[...])
    mix = (ga_ref[...].astype(F32) * ssm + gb_ref[...].astype(F32) * att).astype(BF16)
    out_ref[...] = x_ref[...] + _dot(mix, wo_ref[...])


def _mix(x, ys, o, gates, w_glu, w_ssm_out, w_attn_out, w_o, *, tm):
    m, d = x.shape
    d_ssm = ys.shape[1]
    d_att = o.shape[1]

    def resident(w):
        return pl.BlockSpec(w.shape, lambda i: (0, 0), pipeline_mode=pl.Buffered(1))

    return pl.pallas_call(
        _mix_kernel,
        grid=(m // tm,),
        in_specs=[
            pl.BlockSpec((tm, d), lambda i: (i, 0)),
            pl.BlockSpec((tm, d_ssm), lambda i: (i, 0)),
            pl.BlockSpec((tm, d_att), lambda i: (i, 0)),
            pl.BlockSpec((tm, d), lambda i: (i, 0)),
            pl.BlockSpec((tm, d), lambda i: (i, 1)),
            resident(w_glu), resident(w_ssm_out), resident(w_attn_out), resident(w_o),
        ],
        out_specs=pl.BlockSpec((tm, d), lambda i: (i, 0)),
        out_shape=jax.ShapeDtypeStruct((m, d), F32),
        compiler_params=_params(("parallel",)),
        name="mix",
    )(x, ys, o, gates, gates, w_glu, w_ssm_out, w_attn_out, w_o)


def _rotary_tables(pos, rows):
    half = ROT_DIM // 2
    inv = ROPE_THETA ** (-jnp.arange(half, dtype=F32) / half)
    ang = pos.astype(F32)[:, None] * inv[None, :]
    cos, sin = jnp.cos(ang), jnp.sin(ang)
    t = pos.shape[0]
    zh = jnp.zeros((t, half), F32)
    rest0 = jnp.zeros((t, HEAD_DIM - ROT_DIM), F32)
    cos_c = jnp.concatenate([cos, cos, jnp.ones((t, HEAD_DIM - ROT_DIM), F32)], axis=1)
    s1_c = jnp.concatenate([-sin, zh, rest0], axis=1)
    s2_c = jnp.concatenate([zh, sin, rest0], axis=1)
    return tuple(jnp.broadcast_to(jnp.tile(a, (1, 2)), (rows, 2 * HEAD_DIM)) for a in (cos_c, s1_c, s2_c))


def kernel(x_prompt, x_sample, cache_k, cache_v, state_ssm_re, state_ssm_im, page_table, norm_ffn1, ffn1_w1, ffn1_w3, ffn1_w2, norm_mix, w_in, lam_q1, lam_k1, lam_q2, lam_k2, g_subln, w_attn_out, ssm_a_re, ssm_a_im, ssm_log_dt, ssm_b_re, ssm_b_im, ssm_c_re, ssm_c_im, ssm_d, w_glu, w_ssm_out, w_o, norm_ffn2, ffn2_w1, ffn2_w3, ffn2_w2, norm_final):
    batch, seq, d = x_prompt.shape
    n_dec, dec_seq, _ = x_sample.shape
    assert dec_seq == 1
    depth = cache_k.shape[0]
    n_heads, d_k = cache_k.shape[3], cache_k.shape[4]
    d_v = cache_v.shape[4]
    n_groups, n_state = ssm_a_re.shape[1], ssm_a_re.shape[2]
    d_qk, d_att, d_ssm = n_heads * d_k, n_heads * d_v, n_groups * SSM_GROUP
    past_len = page_table.shape[1] * cache_k.shape[2]
    m_p = batch * seq

    tabs_p = _rotary_tables(jnp.arange(seq, dtype=jnp.int32), seq)
    tabs_s = _rotary_tables(past_len + jnp.arange(dec_seq, dtype=jnp.int32), n_dec)
    row = lambda a: a.reshape(1, -1)
    gfin = row(norm_final)

    h_p = x_prompt.reshape(m_p, d)
    h_s = x_sample.reshape(n_dec, d)
    outs = [[] for _ in range(8)]
    for l in range(depth):
        lam_init = 0.8 - 0.6 * math.exp(-0.3 * l)
        last = l == depth - 1
        win = w_in[l].astype(BF16)
        wglu, wso = w_glu[l].astype(BF16), w_ssm_out[l].astype(BF16)
        wao, wo = w_attn_out[l].astype(BF16), w_o[l].astype(BF16)
        lam4 = jnp.stack([lam_q1[l], lam_k1[l], lam_q2[l], lam_k2[l]])
        gs = row(g_subln[l])
        lay = _s5_layout(ssm_a_re[l], ssm_a_im[l], ssm_log_dt[l], ssm_b_re[l], ssm_b_im[l],
                         ssm_c_re[l], ssm_c_im[l], ssm_d[l])

        def half1(x, tm, w):
            return _ffn(x, row(norm_ffn1[l]), *w, row(norm_mix[l]), norm_out="extra", tm=tm, tf=512)

        def half2(x, tm, w):
            return _ffn(x, row(norm_ffn2[l]), *w, gfin, norm_out="replace" if last else "none", tm=tm, tf=512)

        x1_s, xn_s, *w1_bf = half1(h_s, n_dec, (ffn1_w1[l], ffn1_w3[l], ffn1_w2[l]))
        q_s, k_s, _, v_s, _, u_s, gates_s = _project(xn_s, win, *tabs_s, tm=n_dec,
                                                     d_qk=d_qk, d_v=d_att, d_ssm=d_ssm)
        x1_p, xn_p = half1(h_p, 512, w1_bf)
        q, k, kb, v, vb, u, gates = _project(xn_p, win, *tabs_p, tm=1024,
                                             d_qk=d_qk, d_v=d_att, d_ssm=d_ssm)

        o, o_s = _attention(q, kb, vb, q_s.astype(F32).reshape(n_dec, n_heads, d_k),
                            k_s.reshape(n_dec, n_heads, d_k), v_s.reshape(n_dec, n_heads, d_v),
                            cache_k, cache_v, page_table, lam4, gs,
                            batch=batch, seq=seq, tq=512, layer=l, lam_init=lam_init)

        ys, hre, him = _s5_step(u_s, state_ssm_re[l].reshape(n_dec, n_groups * n_state),
                                state_ssm_im[l].reshape(n_dec, n_groups * n_state), lay)
        x2 = _mix(x1_s, ys, o_s.reshape(n_dec, d_att).astype(BF16), gates_s, wglu, wso, wao, wo, tm=n_dec)
        h_s, *w2_bf = half2(x2, n_dec, (ffn2_w1[l], ffn2_w3[l], ffn2_w2[l]))
        outs[4].append(k_s.reshape(n_dec, dec_seq, n_heads, d_k))
        outs[5].append(v_s.reshape(n_dec, dec_seq, n_heads, d_v))
        outs[6].append(hre.reshape(n_dec, n_groups, n_state))
        outs[7].append(him.reshape(n_dec, n_groups, n_state))

        ys, hre, him = _s5_prompt(u, lay, batch=batch, seq=seq, rows=1024)
        x2 = _mix(x1_p, ys, o, gates, wglu, wso, wao, wo, tm=256)
        (h_p,) = half2(x2, 512, w2_bf)
        outs[0].append(k.reshape(batch, seq, n_heads, d_k))
        outs[1].append(v.reshape(batch, seq, n_heads, d_v))
        outs[2].append(hre.reshape(batch, n_groups, n_state))
        outs[3].append(him.reshape(batch, n_groups, n_state))

    return (h_p.reshape(batch, seq, d), h_s.reshape(n_dec, dec_seq, d)) + tuple(jnp.stack(o) for o in outs)
```

```python
import functools
import math

import jax
import jax.numpy as jnp
from jax import lax
from jax.experimental import pallas as pl
from jax.experimental.pallas import tpu as pltpu

F32 = jnp.float32
BF16 = jnp.bfloat16

HEAD_DIM = 64
ROT_DIM = HEAD_DIM // 4
ROPE_THETA = 500000.0
EPS = 1e-6
SSM_GROUP = 16
SSM_GROUP_BLOCK = 8

V7X_LANES = 128
V7X_SUBLANES = 8
V7X_MXU_WIDTH = 256
MXU_WIDTH = V7X_MXU_WIDTH
V7X_VMEM_BYTES = 64 * 1024 * 1024
VMEM_LIMIT = V7X_VMEM_BYTES * 3 // 4
ATTN_VMEM_LIMIT = V7X_VMEM_BYTES * 7 // 8

NEG_BIG = -1e30
Q_SCALE = HEAD_DIM ** -0.5 * math.log2(math.e)


def _dot(a, b):
    return jnp.dot(a, b, preferred_element_type=F32)


def _dot_nt(a, b):
    return lax.dot_general(a, b, (((1,), (1,)), ((), ())), preferred_element_type=F32)


def _rms(x, g):
    return x * lax.rsqrt(jnp.mean(x * x, axis=-1, keepdims=True) + EPS) * g


def _params(sem):
    return pltpu.CompilerParams(dimension_semantics=sem, vmem_limit_bytes=VMEM_LIMIT)


def _ffn_kernel(x_ref, g_ref, w1_ref, w3_ref, w2_ref, g2_ref, o_ref, *refs, norm_out, emit_bf16):
    f = pl.program_id(1)
    refs = list(refs)
    xn2_ref = refs.pop(0) if norm_out == "extra" else None
    if emit_bf16:
        w1b_ref, w3b_ref, w2b_ref, xn_s = refs
        w1b_ref[...] = w1_ref[...].astype(BF16)
        w3b_ref[...] = w3_ref[...].astype(BF16)
        w2b_ref[...] = (0.5 * w2_ref[...]).astype(BF16)
        w1_ref, w3_ref, w2_ref = w1b_ref, w3b_ref, w2b_ref
    else:
        (xn_s,) = refs

    @pl.when(f == 0)
    def _():
        x = x_ref[...]
        xn_s[...] = _rms(x, g_ref[...]).astype(BF16)
        o_ref[...] = x

    xn = xn_s[...]
    acc = o_ref[...]
    for c in range(w1_ref.shape[1] // MXU_WIDTH):
        cols = slice(c * MXU_WIDTH, (c + 1) * MXU_WIDTH)
        h1 = _dot(xn, w1_ref[:, cols])
        h3 = _dot(xn, w3_ref[:, cols])
        hh = (h1 * jax.nn.sigmoid(h1) * h3).astype(BF16)
        acc = acc + _dot(hh, w2_ref[cols, :])
    o_ref[...] = acc

    if norm_out != "none":
        @pl.when(f == pl.num_programs(1) - 1)
        def _():
            xn2 = _rms(o_ref[...], g2_ref[...])
            if norm_out == "replace":
                o_ref[...] = xn2
            else:
                xn2_ref[...] = xn2.astype(BF16)


def _ffn(x, g, w1, w3, w2, g2, *, norm_out, tm, tf):
    m, d = x.shape
    dff = w1.shape[1]
    emit_bf16 = w1.dtype == F32
    assert not emit_bf16 or m == tm, "each weight tile must be visited exactly once"
    up_spec = pl.BlockSpec((d, tf), lambda i, f: (0, f))
    down_spec = pl.BlockSpec((tf, d), lambda i, f: (f, 0))
    row_spec = pl.BlockSpec((tm, d), lambda i, f: (i, 0))
    gain_spec = pl.BlockSpec((1, d), lambda i, f: (0, 0))
    out_specs = [row_spec]
    out_shape = [jax.ShapeDtypeStruct((m, d), F32)]
    if norm_out == "extra":
        out_specs += [row_spec]
        out_shape += [jax.ShapeDtypeStruct((m, d), BF16)]
    if emit_bf16:
        out_specs += [up_spec, up_spec, down_spec]
        out_shape += [jax.ShapeDtypeStruct(w.shape, BF16) for w in (w1, w3, w2)]
    return pl.pallas_call(
        functools.partial(_ffn_kernel, norm_out=norm_out, emit_bf16=emit_bf16),
        grid=(m // tm, dff // tf),
        in_specs=[row_spec, gain_spec, up_spec, up_spec, down_spec, gain_spec],
        out_specs=out_specs,
        out_shape=out_shape,
        scratch_shapes=[pltpu.VMEM((tm, d), BF16)],
        compiler_params=_params(("parallel", "arbitrary")),
        name="ffn",
    )(x, g, w1, w3, w2, g2)


def _rotate(a, cos_ref, s1_ref, s2_ref):
    cos, s1, s2 = cos_ref[...], s1_ref[...], s2_ref[...]
    half = ROT_DIM // 2
    out = []
    for h in range(a.shape[1] // V7X_LANES):
        blk = a[:, h * V7X_LANES:(h + 1) * V7X_LANES]
        up = pltpu.roll(blk, V7X_LANES - half, 1)
        dn = pltpu.roll(blk, half, 1)
        out.append(blk * cos + up * s1 + dn * s2)
    return jnp.concatenate(out, axis=1)


def _col_chunks(n):
    return [slice(c, c + MXU_WIDTH) for c in range(0, n, MXU_WIDTH)]


def _proj_qk_kernel(xn_ref, w_ref, cos_ref, s1_ref, s2_ref, q_ref, k_ref, kb_ref):
    n = q_ref.shape[1]
    for cols in _col_chunks(n):
        r = _rotate(_dot(xn_ref[...], w_ref[:, cols]), cos_ref, s1_ref, s2_ref)
        q_ref[:, cols] = (r * Q_SCALE).astype(BF16)
    for cols in _col_chunks(n):
        r = _rotate(_dot(xn_ref[...], w_ref[:, slice(n + cols.start, n + cols.stop)]), cos_ref, s1_ref, s2_ref)
        k_ref[:, cols] = r
        kb_ref[:, cols] = r.astype(BF16)


def _proj_vu_kernel(xn_ref, w_ref, v_ref, vb_ref, u_ref):
    n = v_ref.shape[1]
    for cols in _col_chunks(n):
        a = _dot(xn_ref[...], w_ref[:, cols])
        v_ref[:, cols] = a
        vb_ref[:, cols] = a.astype(BF16)
    for cols in _col_chunks(u_ref.shape[1]):
        u_ref[:, cols] = _dot(xn_ref[...], w_ref[:, slice(n + cols.start, n + cols.stop)])


def _proj_gate_kernel(xn_ref, w_ref, s_ref):
    for cols in _col_chunks(s_ref.shape[1]):
        a = _dot(xn_ref[...], w_ref[:, cols])
        s_ref[:, cols] = (0.5 * jnp.tanh(0.5 * a) + 0.5).astype(BF16)


def _project(xn, w_in, cos_t, s1_t, s2_t, *, tm, d_qk, d_v, d_ssm):
    m, d = xn.shape
    tn = 2 * d_qk
    assert d_v + d_ssm == tn and (w_in.shape[1] - 2 * tn) % tn == 0
    n_gate = (w_in.shape[1] - 2 * tn) // tn
    n_t = cos_t.shape[0] // tm
    x_spec = pl.BlockSpec((tm, d), lambda i, j: (i, 0))
    tab_spec = pl.BlockSpec((tm, V7X_LANES), lambda i, j: (i % n_t, 0))

    def row_spec(n):
        return pl.BlockSpec((tm, n), lambda i, j: (i, 0))

    def slab_spec(first):
        return pl.BlockSpec((d, tn), lambda i, j: (0, first + j))

    q, k, kb = pl.pallas_call(
        _proj_qk_kernel,
        grid=(m // tm, 1),
        in_specs=[x_spec, slab_spec(0), tab_spec, tab_spec, tab_spec],
        out_specs=[row_spec(d_qk)] * 3,
        out_shape=[jax.ShapeDtypeStruct((m, d_qk), BF16), jax.ShapeDtypeStruct((m, d_qk), F32),
                   jax.ShapeDtypeStruct((m, d_qk), BF16)],
        compiler_params=_params(("parallel", "arbitrary")),
        name="proj_qk",
    )(xn, w_in, cos_t, s1_t, s2_t)

    v, vb, u = pl.pallas_call(
        _proj_vu_kernel,
        grid=(m // tm, 1),
        in_specs=[x_spec, slab_spec(1)],
        out_specs=[row_spec(d_v), row_spec(d_v), row_spec(d_ssm)],
        out_shape=[jax.ShapeDtypeStruct((m, d_v), F32), jax.ShapeDtypeStruct((m, d_v), BF16),
                   jax.ShapeDtypeStruct((m, d_ssm), F32)],
        compiler_params=_params(("parallel", "arbitrary")),
        name="proj_vu",
    )(xn, w_in)

    gates = pl.pallas_call(
        _proj_gate_kernel,
        grid=(m // tm, n_gate),
        in_specs=[x_spec, slab_spec(2)],
        out_specs=pl.BlockSpec((tm, tn), lambda i, j: (i, j)),
        out_shape=jax.ShapeDtypeStruct((m, n_gate * tn), BF16),
        compiler_params=_params(("parallel", "arbitrary")),
        name="proj_gate",
    )(xn, w_in)
    return q, k, kb, v, vb, u, gates


def _lam(lam_ref, lam_init):
    a = jnp.sum(lam_ref[0:1, :] * lam_ref[1:2, :], axis=-1, keepdims=True)
    b = jnp.sum(lam_ref[2:3, :] * lam_ref[3:4, :], axis=-1, keepdims=True)
    return jnp.exp(a) - jnp.exp(b) + lam_init


def _subln(o, gs, lam_init):
    return _rms(o, gs) * (1.0 - lam_init)


def _split_components(q):
    lane = lax.broadcasted_iota(jnp.int32, q.shape, 1)
    zero = jnp.zeros_like(q)
    return jnp.concatenate([jnp.where(lane < HEAD_DIM, q, zero), jnp.where(lane >= HEAD_DIM, q, zero)], axis=0)


def _softmax_update(s, v_ext, m_s, acc_s):
    m_prev = m_s[...]
    m_new = jnp.maximum(m_prev, jnp.max(s, axis=-1, keepdims=True))
    alpha = jnp.exp2(m_prev - m_new)
    p = jnp.exp2((s - jnp.tile(m_new, (1, s.shape[1] // V7X_LANES))).astype(BF16))
    acc_s[...] = jnp.tile(alpha, (1, 2)) * acc_s[...] + _dot(p, v_ext)
    m_s[...] = m_new


def _prompt_attention(lam_ref, gs_ref, q_ref, k_ref, v_ref, o_ref, qq_s, ve_s, s_s, m_s, acc_s, *, tq, lam_init):
    i = pl.program_id(2)
    dv = v_ref.shape[1]

    @pl.when(i == 0)
    def _():
        ve_s[:, 0:dv] = v_ref[...]
        ve_s[:, dv:] = jnp.ones((ve_s.shape[0], ve_s.shape[1] - dv), BF16)

    qq_s[...] = _split_components(q_ref[...])
    m_s[...] = jnp.full_like(m_s, NEG_BIG)
    acc_s[...] = jnp.zeros_like(acc_s)

    def tile(ref, j):
        return ref[pl.ds(pl.multiple_of(j * tq, tq), tq), :]

    def scores(j, slot):
        s_s[slot] = _dot_nt(qq_s[...], tile(k_ref, j))

    def consume(j, slot, mask=None):
        s = s_s[slot]
        if mask is not None:
            s = jnp.where(mask, s, NEG_BIG)
        _softmax_update(s, tile(ve_s, j), m_s, acc_s)

    scores(0, 0)

    def pair(jj, c):
        j = 2 * jj
        scores(j + 1, 1)
        consume(j, 0)
        scores(j + 2, 0)
        consume(j + 1, 1)
        return c

    lax.fori_loop(0, i // 2, pair, 0)

    row = lax.broadcasted_iota(jnp.int32, s_s.shape[1:], 0)
    row = jnp.where(row >= tq, row - tq, row)
    col = lax.broadcasted_iota(jnp.int32, s_s.shape[1:], 1)
    causal = col <= row

    @pl.when(i % 2 == 1)
    def _():
        scores(i, 1)
        consume(i - 1, 0)
        consume(i, 1, mask=causal)

    @pl.when(i % 2 == 0)
    def _():
        consume(i, 0, mask=causal)

    o1 = acc_s[0:tq, 0:dv] / acc_s[0:tq, dv:]
    o2 = acc_s[tq:2 * tq, 0:dv] / acc_s[tq:2 * tq, dv:]
    o = o1 - _lam(lam_ref, lam_init) * o2
    o_ref[...] = _subln(o, gs_ref[...], lam_init).astype(BF16)


def _decode_attention(lam_ref, gs_ref, q_ref, kn_ref, vn_ref, kc_refs, vc_refs, o_ref, *, lam_init):
    n_heads = q_ref.shape[0]
    page, _, dk = kc_refs[0].shape
    dv = vc_refs[0].shape[2]

    qm = _split_components(q_ref[...])
    qmb = qm.astype(BF16)
    shape = (2 * n_heads, page * n_heads)
    row = lax.broadcasted_iota(jnp.int32, shape, 0)
    col = lax.broadcasted_iota(jnp.int32, shape, 1)
    same_head = ((row ^ col) & (n_heads - 1)) == 0

    kn2 = jnp.concatenate([kn_ref[...], kn_ref[...]], axis=0)
    vn2 = jnp.concatenate([vn_ref[...], vn_ref[...]], axis=0)
    s_self = jnp.sum(qm * kn2, axis=-1, keepdims=True)

    scores = []
    m = s_self
    for kc_ref in kc_refs:
        k2 = kc_ref[...].reshape(page * n_heads, dk).astype(BF16)
        s = jnp.where(same_head, _dot_nt(qmb, k2), NEG_BIG)
        scores.append(s)
        m = jnp.maximum(m, jnp.max(s, axis=-1, keepdims=True))

    p_self = jnp.exp2(s_self - m)
    l = p_self
    acc = p_self * vn2
    for s, vc_ref in zip(scores, vc_refs):
        p = jnp.exp2(s - m)
        l = l + jnp.sum(p, axis=-1, keepdims=True)
        acc = acc + _dot(p.astype(BF16), vc_ref[...].reshape(page * n_heads, dv).astype(BF16))

    o = acc / l
    o = o[0:n_heads, :] - _lam(lam_ref, lam_init) * o[n_heads:2 * n_heads, :]
    o_ref[...] = _subln(o, gs_ref[...], lam_init)


def _attn_kernel(pt_ref, lam_ref, gs_ref, q_ref, k_ref, v_ref, qd_ref, knd_ref, vnd_ref, ck_hbm, cv_hbm,
                 o_ref, od_ref, qq_s, ve_s, s_s, m_s, acc_s, kbuf, vbuf, sem, *, layer, tq, lam_init):
    n_pages = kbuf.shape[1]
    step = (pl.program_id(0) * pl.num_programs(1) + pl.program_id(1)) * pl.num_programs(2) + pl.program_id(2)
    n_steps = pl.num_programs(0) * pl.num_programs(1) * pl.num_programs(2)
    slot = lax.rem(step, 2)

    def page_copies(sample, slot):
        copies = []
        for p in range(n_pages):
            pg = pt_ref[sample, p]
            copies.append(pltpu.make_async_copy(ck_hbm.at[layer, pg], kbuf.at[slot, p], sem.at[slot, 0]))
            copies.append(pltpu.make_async_copy(cv_hbm.at[layer, pg], vbuf.at[slot, p], sem.at[slot, 1]))
        return copies

    @pl.when(step == 0)
    def _():
        for c in page_copies(0, 0):
            c.start()

    @pl.when(step + 1 < n_steps)
    def _():
        for c in page_copies(step + 1, 1 - slot):
            c.start()

    for c in page_copies(step, slot):
        c.wait()

    _decode_attention(lam_ref, gs_ref, qd_ref, knd_ref, vnd_ref,
                      [kbuf.at[slot, p] for p in range(n_pages)], [vbuf.at[slot, p] for p in range(n_pages)],
                      od_ref, lam_init=lam_init)
    _prompt_attention(lam_ref, gs_ref, q_ref, k_ref, v_ref, o_ref, qq_s, ve_s, s_s, m_s, acc_s,
                      tq=tq, lam_init=lam_init)


def _attention(qb, kb, vb, q3, kn3, vn3, cache_k, cache_v, page_table, lam4, gs, *, batch, seq, tq, layer, lam_init):
    m, dq = qb.shape
    n_heads = dq // V7X_LANES
    nq = seq // tq
    n_dec, _, dk = q3.shape
    n_pages = page_table.shape[1]
    page = cache_k.shape[2]
    dv = cache_v.shape[4]
    assert n_heads & (n_heads - 1) == 0 and dv == V7X_LANES
    assert n_dec == batch * n_heads * nq, "one decode sample per prompt-attention grid step"

    def sample(b, h, i):
        return (b * n_heads + h) * nq + i

    def sample_spec(d):
        return pl.BlockSpec((None, n_heads, d), lambda b, h, i, pt: (sample(b, h, i), 0, 0))

    tile_spec = pl.BlockSpec((tq, V7X_LANES), lambda b, h, i, pt: (b * nq + i, h))
    head_spec = pl.BlockSpec((seq, V7X_LANES), lambda b, h, i, pt: (b, h))
    hbm_spec = pl.BlockSpec(memory_space=pl.ANY)
    grid_spec = pltpu.PrefetchScalarGridSpec(
        num_scalar_prefetch=1,
        grid=(batch, n_heads, nq),
        in_specs=[
            pl.BlockSpec(lam4.shape, lambda b, h, i, pt: (0, 0)),
            pl.BlockSpec((1, dv), lambda b, h, i, pt: (0, 0)),
            tile_spec, head_spec, head_spec,
            sample_spec(dk), sample_spec(dk), sample_spec(dv),
            hbm_spec, hbm_spec,
        ],
        out_specs=[tile_spec, sample_spec(dv)],
        scratch_shapes=[pltpu.VMEM((2 * tq, V7X_LANES), BF16), pltpu.VMEM((seq, 2 * V7X_LANES), BF16),
                        pltpu.VMEM((2, 2 * tq, tq), F32), pltpu.VMEM((2 * tq, V7X_LANES), F32),
                        pltpu.VMEM((2 * tq, 2 * V7X_LANES), F32),
                        pltpu.VMEM((2, n_pages, page, n_heads, dk), F32),
                        pltpu.VMEM((2, n_pages, page, n_heads, dv), F32),
                        pltpu.SemaphoreType.DMA((2, 2))],
    )
    return pl.pallas_call(
        functools.partial(_attn_kernel, layer=layer, tq=tq, lam_init=lam_init),
        grid_spec=grid_spec,
        out_shape=[jax.ShapeDtypeStruct((m, dq), BF16), jax.ShapeDtypeStruct((n_dec, n_heads, dv), F32)],
        compiler_params=pltpu.CompilerParams(dimension_semantics=("arbitrary",) * 3,
                                             vmem_limit_bytes=ATTN_VMEM_LIMIT),
        name="attention",
    )(page_table, lam4, gs, qb, kb, vb, q3, kn3, vn3, cache_k, cache_v)


def _s5_discretize(are, aim, ldt):
    dt = jnp.exp(ldt)
    mag = jnp.exp(dt * are)
    lbr = mag * jnp.cos(dt * aim)
    lbi = mag * jnp.sin(dt * aim)
    nr = lbr - 1.0
    den = are * are + aim * aim
    cr = (nr * are + lbi * aim) / den
    ci = (lbi * are - nr * aim) / den
    return lbr, lbi, cr, ci


def _s5_input_weight(cr, ci, bre, bim):
    return jnp.concatenate([cr * bre - ci * bim, cr * bim + ci * bre], axis=1).astype(BF16)


def _s5_readout(hr, hi, cre_ref, cim_ref, d_ref, u):
    return (_dot_nt(hr.astype(BF16), cre_ref[...].astype(BF16))
            - _dot_nt(hi.astype(BF16), cim_ref[...].astype(BF16)) + d_ref[...] * u)


def _s5_prompt_kernel(are_ref, aim_ref, ldt_ref, bre_ref, bim_ref, cre_ref, cim_ref, d_ref, u_ref,
                      y_ref, hre_ref, him_ref, up_s, bur_s, bui_s, yp_s, pad_s, wb_s, lam_s, car_s, *, seg):
    ch = pl.program_id(2)
    n_st = bur_s.shape[1]
    sub = V7X_SUBLANES

    @pl.when(ch == 0)
    def _():
        lbr, lbi, cr, ci = _s5_discretize(are_ref[...], aim_ref[...], ldt_ref[...])
        wb_s[...] = _s5_input_weight(cr, ci, bre_ref[...], bim_ref[...])
        lam_s[0:1, :] = lbr
        lam_s[1:2, :] = lbi
        pr, pi = lbr, lbi
        for _ in range(int(math.log2(seg))):
            pr, pi = pr * pr - pi * pi, 2.0 * pr * pi
        lam_s[2:3, :] = pr
        lam_s[3:4, :] = pi
        car_s[...] = jnp.zeros_like(car_s)

    pitch = seg + sub
    for r in range(sub):
        pad_s[r * pitch:r * pitch + seg, :] = u_ref[r * seg:(r + 1) * seg, :]

    def permute(t, c):
        up_s[pl.ds(pl.multiple_of(t * sub, sub), sub), :] = pad_s[pl.ds(t, sub, stride=pitch), :]
        return c

    lax.fori_loop(0, seg, permute, 0, unroll=8)

    bu = _dot(up_s[...].astype(BF16), wb_s[...])
    bur_s[...] = bu[:, :n_st]
    bui_s[...] = bu[:, n_st:]

    lbr = jnp.broadcast_to(lam_s[0:1, :], (sub, n_st))
    lbi = jnp.broadcast_to(lam_s[1:2, :], (sub, n_st))

    def advance(t, c):
        sr, si = c
        r0 = pl.multiple_of(t * sub, sub)
        return (lbr * sr - lbi * si + bur_s[pl.ds(r0, sub), :],
                lbr * si + lbi * sr + bui_s[pl.ds(r0, sub), :])

    zero = jnp.zeros((sub, n_st), F32)
    er, ei = lax.fori_loop(0, seg, advance, (zero, zero), unroll=4)

    psr, psi = lam_s[2:3, :], lam_s[3:4, :]
    hr, hi = car_s[0:1, :], car_s[1:2, :]
    rows_r, rows_i = [], []
    for r in range(sub):
        rows_r.append(hr)
        rows_i.append(hi)
        hr, hi = (er[r:r + 1, :] + psr * hr - psi * hi, ei[r:r + 1, :] + psr * hi + psi * hr)
    car_s[0:1, :] = hr
    car_s[1:2, :] = hi

    def advance_store(t, c):
        sr, si = advance(t, c)
        r0 = pl.multiple_of(t * sub, sub)
        bur_s[pl.ds(r0, sub), :] = sr
        bui_s[pl.ds(r0, sub), :] = si
        return sr, si

    lax.fori_loop(0, seg, advance_store,
                  (jnp.concatenate(rows_r, axis=0), jnp.concatenate(rows_i, axis=0)), unroll=4)

    yp_s[...] = _s5_readout(bur_s[...], bui_s[...], cre_ref, cim_ref, d_ref, up_s[...])

    def unpermute(t, c):
        pad_s[pl.ds(t, sub, stride=pitch), :] = yp_s[pl.ds(pl.multiple_of(t * sub, sub), sub), :]
        return c

    lax.fori_loop(0, seg, unpermute, 0, unroll=8)
    for r in range(sub):
        y_ref[r * seg:(r + 1) * seg, :] = pad_s[r * pitch:r * pitch + seg, :]

    @pl.when(ch == pl.num_programs(2) - 1)
    def _():
        hre_ref[...] = hr
        him_ref[...] = hi


def _s5_prompt(u, lay, *, batch, seq, rows):
    m, d_ssm = u.shape
    are, aim, ldt, bre, bim, cre, cim, dsk = lay
    n_gb, n_ch_lanes, n_st = bre.shape
    n_ch = seq // rows
    seg = rows // V7X_SUBLANES
    lane_spec = pl.BlockSpec((None, 1, n_st), lambda b, g, c: (g, 0, 0))
    mat_spec = pl.BlockSpec((None, n_ch_lanes, n_st), lambda b, g, c: (g, 0, 0))
    row_spec = pl.BlockSpec((rows, n_ch_lanes), lambda b, g, c: (b * n_ch + c, g))
    st_spec = pl.BlockSpec((None, None, 1, n_st), lambda b, g, c: (b, g, 0, 0))
    st_shape = jax.ShapeDtypeStruct((batch, n_gb, 1, n_st), F32)
    return pl.pallas_call(
        functools.partial(_s5_prompt_kernel, seg=seg),
        grid=(batch, n_gb, n_ch),
        in_specs=[lane_spec, lane_spec, lane_spec, mat_spec, mat_spec, mat_spec, mat_spec,
                  pl.BlockSpec((1, n_ch_lanes), lambda b, g, c: (0, g)), row_spec],
        out_specs=[row_spec, st_spec, st_spec],
        out_shape=[jax.ShapeDtypeStruct((m, d_ssm), F32), st_shape, st_shape],
        scratch_shapes=[pltpu.VMEM((rows, n_ch_lanes), F32), pltpu.VMEM((rows, n_st), F32),
                        pltpu.VMEM((rows, n_st), F32), pltpu.VMEM((rows, n_ch_lanes), F32),
                        pltpu.VMEM((rows + V7X_SUBLANES * V7X_SUBLANES, n_ch_lanes), F32),
                        pltpu.VMEM((n_ch_lanes, 2 * n_st), BF16), pltpu.VMEM((4, n_st), F32),
                        pltpu.VMEM((2, n_st), F32)],
        compiler_params=_params(("parallel", "parallel", "arbitrary")),
        name="s5_prompt",
    )(are, aim, ldt, bre, bim, cre, cim, dsk, u)


def _s5_step_kernel(are_ref, aim_ref, ldt_ref, bre_ref, bim_ref, cre_ref, cim_ref, d_ref, u_ref,
                    h0r_ref, h0i_ref, y_ref, hre_ref, him_ref):
    n_st = h0r_ref.shape[1]
    lbr, lbi, cr, ci = _s5_discretize(are_ref[...], aim_ref[...], ldt_ref[...])
    u = u_ref[...]
    bu = _dot(u.astype(BF16), _s5_input_weight(cr, ci, bre_ref[...], bim_ref[...]))
    h0r, h0i = h0r_ref[...], h0i_ref[...]
    hr = lbr * h0r - lbi * h0i + bu[:, :n_st]
    hi = lbr * h0i + lbi * h0r + bu[:, n_st:]
    hre_ref[...] = hr
    him_ref[...] = hi
    y_ref[...] = _s5_readout(hr, hi, cre_ref, cim_ref, d_ref, u)


def _s5_step(u, h0r, h0i, lay):
    m, d_ssm = u.shape
    are, aim, ldt, bre, bim, cre, cim, dsk = lay
    n_gb, n_ch_lanes, n_st = bre.shape
    lane_spec = pl.BlockSpec((None, 1, n_st), lambda g: (g, 0, 0))
    mat_spec = pl.BlockSpec((None, n_ch_lanes, n_st), lambda g: (g, 0, 0))
    row_spec = pl.BlockSpec((m, n_ch_lanes), lambda g: (0, g))
    st_spec = pl.BlockSpec((m, n_st), lambda g: (0, g))
    st_shape = jax.ShapeDtypeStruct(h0r.shape, F32)
    return pl.pallas_call(
        _s5_step_kernel,
        grid=(n_gb,),
        in_specs=[lane_spec, lane_spec, lane_spec, mat_spec, mat_spec, mat_spec, mat_spec,
                  pl.BlockSpec((1, n_ch_lanes), lambda g: (0, g)), row_spec, st_spec, st_spec],
        out_specs=[row_spec, st_spec, st_spec],
        out_shape=[jax.ShapeDtypeStruct((m, d_ssm), F32), st_shape, st_shape],
        compiler_params=_params(("parallel",)),
        name="s5_step",
    )(are, aim, ldt, bre, bim, cre, cim, dsk, u, h0r, h0i)


def _s5_layout(a_re, a_im, log_dt, b_re, b_im, c_re, c_im, d_skip):
    n_g, n_st = a_re.shape
    gl = SSM_GROUP_BLOCK
    n_gb = n_g // gl
    same = jnp.eye(gl, dtype=jnp.bool_)[None, :, None, :, None]

    def lanes(a):
        return a.reshape(n_gb, 1, gl * n_st)

    def block_diag(w):
        w5 = w.reshape(n_gb, gl, SSM_GROUP, 1, n_st)
        return jnp.where(same, w5, 0.0).reshape(n_gb, gl * SSM_GROUP, gl * n_st)

    return (lanes(a_re), lanes(a_im), lanes(jnp.broadcast_to(log_dt[:, None], (n_g, n_st))),
            block_diag(b_re.transpose(0, 2, 1)), block_diag(b_im.transpose(0, 2, 1)),
            block_diag(c_re), block_diag(c_im), d_skip.reshape(1, n_g * SSM_GROUP))


def _mix_kernel(x_ref, ys_ref, o_ref, ga_ref, gb_ref, wglu_ref, wso_ref, wao_ref, wo_ref, out_ref):
    ys = ys_ref[...]
    ya = 0.5 * ys * (1.0 + lax.erf(ys * (2.0 ** -0.5)))
    glu = (ya * jax.nn.sigmoid(_dot(ya.astype(BF16), wglu_ref[...]))).astype(BF16)
    ssm = _dot(glu, wso_ref[...])
    att = _dot(o_ref[...], wao_ref[...])
    mix = (ga_ref[...].astype(F32) * ssm + gb_ref[...].astype(F32) * att).astype(BF16)
    out_ref[...] = x_ref[...] + _dot(mix, wo_ref[...])


def _mix(x, ys, o, gates, w_glu, w_ssm_out, w_attn_out, w_o, *, tm):
    m, d = x.shape
    d_ssm = ys.shape[1]
    d_att = o.shape[1]

    def resident(w):
        return pl.BlockSpec(w.shape, lambda i: (0, 0), pipeline_mode=pl.Buffered(1))

    return pl.pallas_call(
        _mix_kernel,
        grid=(m // tm,),
        in_specs=[
            pl.BlockSpec((tm, d), lambda i: (i, 0)),
            pl.BlockSpec((tm, d_ssm), lambda i: (i, 0)),
            pl.BlockSpec((tm, d_att), lambda i: (i, 0)),
            pl.BlockSpec((tm, d), lambda i: (i, 0)),
            pl.BlockSpec((tm, d), lambda i: (i, 1)),
            resident(w_glu), resident(w_ssm_out), resident(w_attn_out), resident(w_o),
        ],
        out_specs=pl.BlockSpec((tm, d), lambda i: (i, 0)),
        out_shape=jax.ShapeDtypeStruct((m, d), F32),
        compiler_params=_params(("parallel",)),
        name="mix",
    )(x, ys, o, gates, gates, w_glu, w_ssm_out, w_attn_out, w_o)


def _rotary_tables(pos, rows):
    half = ROT_DIM // 2
    inv = ROPE_THETA ** (-jnp.arange(half, dtype=F32) / half)
    ang = pos.astype(F32)[:, None] * inv[None, :]
    cos, sin = jnp.cos(ang), jnp.sin(ang)
    t = pos.shape[0]
    zh = jnp.zeros((t, half), F32)
    rest0 = jnp.zeros((t, HEAD_DIM - ROT_DIM), F32)
    cos_c = jnp.concatenate([cos, cos, jnp.ones((t, HEAD_DIM - ROT_DIM), F32)], axis=1)
    s1_c = jnp.concatenate([-sin, zh, rest0], axis=1)
    s2_c = jnp.concatenate([zh, sin, rest0], axis=1)
    return tuple(jnp.broadcast_to(jnp.tile(a, (1, 2)), (rows, 2 * HEAD_DIM)) for a in (cos_c, s1_c, s2_c))


def kernel(x_prompt, x_sample, cache_k, cache_v, state_ssm_re, state_ssm_im, page_table, norm_ffn1, ffn1_w1, ffn1_w3, ffn1_w2, norm_mix, w_in, lam_q1, lam_k1, lam_q2, lam_k2, g_subln, w_attn_out, ssm_a_re, ssm_a_im, ssm_log_dt, ssm_b_re, ssm_b_im, ssm_c_re, ssm_c_im, ssm_d, w_glu, w_ssm_out, w_o, norm_ffn2, ffn2_w1, ffn2_w3, ffn2_w2, norm_final):
    batch, seq, d = x_prompt.shape
    n_dec, dec_seq, _ = x_sample.shape
    assert dec_seq == 1
    depth = cache_k.shape[0]
    n_heads, d_k = cache_k.shape[3], cache_k.shape[4]
    d_v = cache_v.shape[4]
    n_groups, n_state = ssm_a_re.shape[1], ssm_a_re.shape[2]
    d_qk, d_att, d_ssm = n_heads * d_k, n_heads * d_v, n_groups * SSM_GROUP
    past_len = page_table.shape[1] * cache_k.shape[2]
    m_p = batch * seq

    tabs_p = _rotary_tables(jnp.arange(seq, dtype=jnp.int32), seq)
    tabs_s = _rotary_tables(past_len + jnp.arange(dec_seq, dtype=jnp.int32), n_dec)
    row = lambda a: a.reshape(1, -1)
    gfin = row(norm_final)

    h_p = x_prompt.reshape(m_p, d)
    h_s = x_sample.reshape(n_dec, d)
    outs = [[] for _ in range(8)]
    for l in range(depth):
        lam_init = 0.8 - 0.6 * math.exp(-0.3 * l)
        last = l == depth - 1
        win = w_in[l].astype(BF16)
        wglu, wso = w_glu[l].astype(BF16), w_ssm_out[l].astype(BF16)
        wao, wo = w_attn_out[l].astype(BF16), w_o[l].astype(BF16)
        lam4 = jnp.stack([lam_q1[l], lam_k1[l], lam_q2[l], lam_k2[l]])
        gs = row(g_subln[l])
        lay = _s5_layout(ssm_a_re[l], ssm_a_im[l], ssm_log_dt[l], ssm_b_re[l], ssm_b_im[l],
                         ssm_c_re[l], ssm_c_im[l], ssm_d[l])

        def half1(x, tm, w):
            return _ffn(x, row(norm_ffn1[l]), *w, row(norm_mix[l]), norm_out="extra", tm=tm, tf=512)

        def half2(x, tm, w):
            return _ffn(x, row(norm_ffn2[l]), *w, gfin, norm_out="replace" if last else "none", tm=tm, tf=512)

        x1_s, xn_s, *w1_bf = half1(h_s, n_dec, (ffn1_w1[l], ffn1_w3[l], ffn1_w2[l]))
        q_s, k_s, _, v_s, _, u_s, gates_s = _project(xn_s, win, *tabs_s, tm=n_dec,
                                                     d_qk=d_qk, d_v=d_att, d_ssm=d_ssm)
        x1_p, xn_p = half1(h_p, 512, w1_bf)
        q, k, kb, v, vb, u, gates = _project(xn_p, win, *tabs_p, tm=1024,
                                             d_qk=d_qk, d_v=d_att, d_ssm=d_ssm)

        o, o_s = _attention(q, kb, vb, q_s.astype(F32).reshape(n_dec, n_heads, d_k),
                            k_s.reshape(n_dec, n_heads, d_k), v_s.reshape(n_dec, n_heads, d_v),
                            cache_k, cache_v, page_table, lam4, gs,
                            batch=batch, seq=seq, tq=512, layer=l, lam_init=lam_init)

        ys, hre, him = _s5_step(u_s, state_ssm_re[l].reshape(n_dec, n_groups * n_state),
                                state_ssm_im[l].reshape(n_dec, n_groups * n_state), lay)
        x2 = _mix(x1_s, ys, o_s.reshape(n_dec, d_att).astype(BF16), gates_s, wglu, wso, wao, wo, tm=n_dec)
        h_s, *w2_bf = half2(x2, n_dec, (ffn2_w1[l], ffn2_w3[l], ffn2_w2[l]))
        outs[4].append(k_s.reshape(n_dec, dec_seq, n_heads, d_k))
        outs[5].append(v_s.reshape(n_dec, dec_seq, n_heads, d_v))
        outs[6].append(hre.reshape(n_dec, n_groups, n_state))
        outs[7].append(him.reshape(n_dec, n_groups, n_state))

        ys, hre, him = _s5_prompt(u, lay, batch=batch, seq=seq, rows=2048)
        x2 = _mix(x1_p, ys, o, gates, wglu, wso, wao, wo, tm=256)
        (h_p,) = half2(x2, 512, w2_bf)
        outs[0].append(k.reshape(batch, seq, n_heads, d_k))
        outs[1].append(v.reshape(batch, seq, n_heads, d_v))
        outs[2].append(hre.reshape(batch, n_groups, n_state))
        outs[3].append(him.reshape(batch, n_groups, n_state))

    return (h_p.reshape(batch, seq, d), h_s.reshape(n_dec, dec_seq, d)) + tuple(jnp.stack(o) for o in outs)
```

```python
import functools
import math

import jax
import jax.numpy as jnp
from jax import lax
from jax.experimental import pallas as pl
from jax.experimental.pallas import tpu as pltpu

F32 = jnp.float32
BF16 = jnp.bfloat16

HEAD_DIM = 64
ROT_DIM = HEAD_DIM // 4
ROPE_THETA = 500000.0
EPS = 1e-6
SSM_GROUP = 16
SSM_GROUP_BLOCK = 8

V7X_LANES = 128
V7X_SUBLANES = 8
V7X_MXU_WIDTH = 256
MXU_WIDTH = V7X_MXU_WIDTH
V7X_VMEM_BYTES = 64 * 1024 * 1024
VMEM_LIMIT = V7X_VMEM_BYTES * 3 // 4
ATTN_VMEM_LIMIT = V7X_VMEM_BYTES * 7 // 8

NEG_BIG = -1e30
Q_SCALE = HEAD_DIM ** -0.5 * math.log2(math.e)


def _dot(a, b):
    return jnp.dot(a, b, preferred_element_type=F32)


def _dot_nt(a, b):
    return lax.dot_general(a, b, (((1,), (1,)), ((), ())), preferred_element_type=F32)


def _rms(x, g):
    return x * lax.rsqrt(jnp.mean(x * x, axis=-1, keepdims=True) + EPS) * g


def _params(sem):
    return pltpu.CompilerParams(dimension_semantics=sem, vmem_limit_bytes=VMEM_LIMIT)


def _ffn_kernel(x_ref, g_ref, w1_ref, w3_ref, w2_ref, g2_ref, o_ref, *refs, norm_out, emit_bf16):
    f = pl.program_id(1)
    refs = list(refs)
    xn2_ref = refs.pop(0) if norm_out == "extra" else None
    if emit_bf16:
        w1b_ref, w3b_ref, w2b_ref, xn_s = refs
        w1b_ref[...] = w1_ref[...].astype(BF16)
        w3b_ref[...] = w3_ref[...].astype(BF16)
        w2b_ref[...] = (0.5 * w2_ref[...]).astype(BF16)
        w1_ref, w3_ref, w2_ref = w1b_ref, w3b_ref, w2b_ref
    else:
        (xn_s,) = refs

    @pl.when(f == 0)
    def _():
        x = x_ref[...]
        xn_s[...] = _rms(x, g_ref[...]).astype(BF16)
        o_ref[...] = x

    xn = xn_s[...]
    acc = o_ref[...]
    for c in range(w1_ref.shape[1] // MXU_WIDTH):
        cols = slice(c * MXU_WIDTH, (c + 1) * MXU_WIDTH)
        h1 = _dot(xn, w1_ref[:, cols])
        h3 = _dot(xn, w3_ref[:, cols])
        hh = (h1 * jax.nn.sigmoid(h1) * h3).astype(BF16)
        acc = acc + _dot(hh, w2_ref[cols, :])
    o_ref[...] = acc

    if norm_out != "none":
        @pl.when(f == pl.num_programs(1) - 1)
        def _():
            xn2 = _rms(o_ref[...], g2_ref[...])
            if norm_out == "replace":
                o_ref[...] = xn2
            else:
                xn2_ref[...] = xn2.astype(BF16)


def _ffn(x, g, w1, w3, w2, g2, *, norm_out, tm, tf):
    m, d = x.shape
    dff = w1.shape[1]
    emit_bf16 = w1.dtype == F32
    assert not emit_bf16 or m == tm, "each weight tile must be visited exactly once"
    up_spec = pl.BlockSpec((d, tf), lambda i, f: (0, f))
    down_spec = pl.BlockSpec((tf, d), lambda i, f: (f, 0))
    row_spec = pl.BlockSpec((tm, d), lambda i, f: (i, 0))
    gain_spec = pl.BlockSpec((1, d), lambda i, f: (0, 0))
    out_specs = [row_spec]
    out_shape = [jax.ShapeDtypeStruct((m, d), F32)]
    if norm_out == "extra":
        out_specs += [row_spec]
        out_shape += [jax.ShapeDtypeStruct((m, d), BF16)]
    if emit_bf16:
        out_specs += [up_spec, up_spec, down_spec]
        out_shape += [jax.ShapeDtypeStruct(w.shape, BF16) for w in (w1, w3, w2)]
    return pl.pallas_call(
        functools.partial(_ffn_kernel, norm_out=norm_out, emit_bf16=emit_bf16),
        grid=(m // tm, dff // tf),
        in_specs=[row_spec, gain_spec, up_spec, up_spec, down_spec, gain_spec],
        out_specs=out_specs,
        out_shape=out_shape,
        scratch_shapes=[pltpu.VMEM((tm, d), BF16)],
        compiler_params=_params(("parallel", "arbitrary")),
        name="ffn",
    )(x, g, w1, w3, w2, g2)


def _rotate(a, cos_ref, s1_ref, s2_ref):
    cos, s1, s2 = cos_ref[...], s1_ref[...], s2_ref[...]
    half = ROT_DIM // 2
    out = []
    for h in range(a.shape[1] // V7X_LANES):
        blk = a[:, h * V7X_LANES:(h + 1) * V7X_LANES]
        up = pltpu.roll(blk, V7X_LANES - half, 1)
        dn = pltpu.roll(blk, half, 1)
        out.append(blk * cos + up * s1 + dn * s2)
    return jnp.concatenate(out, axis=1)


def _col_chunks(n):
    return [slice(c, c + MXU_WIDTH) for c in range(0, n, MXU_WIDTH)]


def _bf16_slab(w_ref, wb_refs):
    if not wb_refs:
        return w_ref
    (wb_ref,) = wb_refs
    wb_ref[...] = w_ref[...].astype(BF16)
    return wb_ref


def _proj_qk_kernel(xn_ref, w_ref, cos_ref, s1_ref, s2_ref, q_ref, k_ref, kb_ref, *wb_refs):
    w_ref = _bf16_slab(w_ref, wb_refs)
    n = q_ref.shape[1]
    for cols in _col_chunks(n):
        r = _rotate(_dot(xn_ref[...], w_ref[:, cols]), cos_ref, s1_ref, s2_ref)
        q_ref[:, cols] = (r * Q_SCALE).astype(BF16)
    for cols in _col_chunks(n):
        r = _rotate(_dot(xn_ref[...], w_ref[:, slice(n + cols.start, n + cols.stop)]), cos_ref, s1_ref, s2_ref)
        k_ref[:, cols] = r
        kb_ref[:, cols] = r.astype(BF16)


def _proj_vu_kernel(xn_ref, w_ref, v_ref, vb_ref, u_ref, *wb_refs):
    w_ref = _bf16_slab(w_ref, wb_refs)
    n = v_ref.shape[1]
    for cols in _col_chunks(n):
        a = _dot(xn_ref[...], w_ref[:, cols])
        v_ref[:, cols] = a
        vb_ref[:, cols] = a.astype(BF16)
    for cols in _col_chunks(u_ref.shape[1]):
        u_ref[:, cols] = _dot(xn_ref[...], w_ref[:, slice(n + cols.start, n + cols.stop)])


def _proj_gate_kernel(xn_ref, w_ref, s_ref, *wb_refs):
    w_ref = _bf16_slab(w_ref, wb_refs)
    for cols in _col_chunks(s_ref.shape[1]):
        a = _dot(xn_ref[...], w_ref[:, cols])
        s_ref[:, cols] = (0.5 * jnp.tanh(0.5 * a) + 0.5).astype(BF16)


def _project(xn, w_in, cos_t, s1_t, s2_t, *, tm, d_qk, d_v, d_ssm):
    m, d = xn.shape
    tn = 2 * d_qk
    emit_bf16 = not isinstance(w_in, tuple)
    if emit_bf16:
        assert m == tm, "each weight slab must be visited exactly once"
        assert d_v + d_ssm == tn and (w_in.shape[1] - 2 * tn) % tn == 0
        n_gate = (w_in.shape[1] - 2 * tn) // tn
        slabs = (w_in,) * 3
        first = (0, 1, 2)
    else:
        slabs = w_in
        n_gate = slabs[2].shape[1] // tn
        first = (0, 0, 0)
    n_t = cos_t.shape[0] // tm
    x_spec = pl.BlockSpec((tm, d), lambda i, j: (i, 0))
    tab_spec = pl.BlockSpec((tm, V7X_LANES), lambda i, j: (i % n_t, 0))

    def row_spec(n):
        return pl.BlockSpec((tm, n), lambda i, j: (i, 0))

    def slab_spec(c):
        mode = dict(pipeline_mode=pl.Buffered(1)) if emit_bf16 else {}
        return pl.BlockSpec((d, tn), lambda i, j: (0, first[c] + j), **mode)

    def with_slab_out(out_specs, out_shape, n_slabs):
        if emit_bf16:
            out_specs = out_specs + [pl.BlockSpec((d, tn), lambda i, j: (0, j))]
            out_shape = out_shape + [jax.ShapeDtypeStruct((d, n_slabs * tn), BF16)]
        return dict(out_specs=out_specs, out_shape=out_shape)

    q, k, kb, *w_qk = pl.pallas_call(
        _proj_qk_kernel,
        grid=(m // tm, 1),
        in_specs=[x_spec, slab_spec(0), tab_spec, tab_spec, tab_spec],
        **with_slab_out([row_spec(d_qk)] * 3,
                        [jax.ShapeDtypeStruct((m, d_qk), BF16), jax.ShapeDtypeStruct((m, d_qk), F32),
                         jax.ShapeDtypeStruct((m, d_qk), BF16)], 1),
        compiler_params=_params(("parallel", "arbitrary")),
        name="proj_qk",
    )(xn, slabs[0], cos_t, s1_t, s2_t)

    v, vb, u, *w_vu = pl.pallas_call(
        _proj_vu_kernel,
        grid=(m // tm, 1),
        in_specs=[x_spec, slab_spec(1)],
        **with_slab_out([row_spec(d_v), row_spec(d_v), row_spec(d_ssm)],
                        [jax.ShapeDtypeStruct((m, d_v), F32), jax.ShapeDtypeStruct((m, d_v), BF16),
                         jax.ShapeDtypeStruct((m, d_ssm), F32)], 1),
        compiler_params=_params(("parallel", "arbitrary")),
        name="proj_vu",
    )(xn, slabs[1])

    gates, *w_gate = pl.pallas_call(
        _proj_gate_kernel,
        grid=(m // tm, n_gate),
        in_specs=[x_spec, slab_spec(2)],
        **with_slab_out([pl.BlockSpec((tm, tn), lambda i, j: (i, j))],
                        [jax.ShapeDtypeStruct((m, n_gate * tn), BF16)], n_gate),
        compiler_params=_params(("parallel", "arbitrary")),
        name="proj_gate",
    )(xn, slabs[2])
    outs = (q, k, kb, v, vb, u, gates)
    return outs + ((w_qk[0], w_vu[0], w_gate[0]),) if emit_bf16 else outs


def _lam(lam_ref, lam_init):
    a = jnp.sum(lam_ref[0:1, :] * lam_ref[1:2, :], axis=-1, keepdims=True)
    b = jnp.sum(lam_ref[2:3, :] * lam_ref[3:4, :], axis=-1, keepdims=True)
    return jnp.exp(a) - jnp.exp(b) + lam_init


def _subln(o, gs, lam_init):
    return _rms(o, gs) * (1.0 - lam_init)


def _split_components(q):
    lane = lax.broadcasted_iota(jnp.int32, q.shape, 1)
    zero = jnp.zeros_like(q)
    return jnp.concatenate([jnp.where(lane < HEAD_DIM, q, zero), jnp.where(lane >= HEAD_DIM, q, zero)], axis=0)


def _softmax_update(s, v_ext, m_s, acc_s):
    m_prev = m_s[...]
    m_new = jnp.maximum(m_prev, jnp.max(s, axis=-1, keepdims=True))
    alpha = jnp.exp2(m_prev - m_new)
    p = jnp.exp2((s - jnp.tile(m_new, (1, s.shape[1] // V7X_LANES))).astype(BF16))
    acc_s[...] = jnp.tile(alpha, (1, 2)) * acc_s[...] + _dot(p, v_ext)
    m_s[...] = m_new


def _prompt_attention(lam_ref, gs_ref, q_ref, k_ref, v_ref, o_ref, qq_s, ve_s, s_s, m_s, acc_s, *, tq, lam_init):
    i = pl.program_id(2)
    dv = v_ref.shape[1]

    @pl.when(i == 0)
    def _():
        ve_s[:, 0:dv] = v_ref[...]
        ve_s[:, dv:] = jnp.ones((ve_s.shape[0], ve_s.shape[1] - dv), BF16)

    qq_s[...] = _split_components(q_ref[...])
    m_s[...] = jnp.full_like(m_s, NEG_BIG)
    acc_s[...] = jnp.zeros_like(acc_s)

    def tile(ref, j):
        return ref[pl.ds(pl.multiple_of(j * tq, tq), tq), :]

    def scores(j, slot):
        s_s[slot] = _dot_nt(qq_s[...], tile(k_ref, j))

    def consume(j, slot, mask=None):
        s = s_s[slot]
        if mask is not None:
            s = jnp.where(mask, s, NEG_BIG)
        _softmax_update(s, tile(ve_s, j), m_s, acc_s)

    scores(0, 0)

    def pair(jj, c):
        j = 2 * jj
        scores(j + 1, 1)
        consume(j, 0)
        scores(j + 2, 0)
        consume(j + 1, 1)
        return c

    lax.fori_loop(0, i // 2, pair, 0)

    row = lax.broadcasted_iota(jnp.int32, s_s.shape[1:], 0)
    row = jnp.where(row >= tq, row - tq, row)
    col = lax.broadcasted_iota(jnp.int32, s_s.shape[1:], 1)
    causal = col <= row

    @pl.when(i % 2 == 1)
    def _():
        scores(i, 1)
        consume(i - 1, 0)
        consume(i, 1, mask=causal)

    @pl.when(i % 2 == 0)
    def _():
        consume(i, 0, mask=causal)

    o1 = acc_s[0:tq, 0:dv] / acc_s[0:tq, dv:]
    o2 = acc_s[tq:2 * tq, 0:dv] / acc_s[tq:2 * tq, dv:]
    o = o1 - _lam(lam_ref, lam_init) * o2
    o_ref[...] = _subln(o, gs_ref[...], lam_init).astype(BF16)


def _decode_attention(lam_ref, gs_ref, q_ref, kn_ref, vn_ref, kc_refs, vc_refs, o_ref, *, lam_init):
    n_heads = q_ref.shape[0]
    page, _, dk = kc_refs[0].shape
    dv = vc_refs[0].shape[2]

    qm = _split_components(q_ref[...])
    qmb = qm.astype(BF16)
    shape = (2 * n_heads, page * n_heads)
    row = lax.broadcasted_iota(jnp.int32, shape, 0)
    col = lax.broadcasted_iota(jnp.int32, shape, 1)
    same_head = ((row ^ col) & (n_heads - 1)) == 0

    kn2 = jnp.concatenate([kn_ref[...], kn_ref[...]], axis=0)
    vn2 = jnp.concatenate([vn_ref[...], vn_ref[...]], axis=0)
    s_self = jnp.sum(qm * kn2, axis=-1, keepdims=True)

    scores = []
    m = s_self
    for kc_ref in kc_refs:
        k2 = kc_ref[...].reshape(page * n_heads, dk).astype(BF16)
        s = jnp.where(same_head, _dot_nt(qmb, k2), NEG_BIG)
        scores.append(s)
        m = jnp.maximum(m, jnp.max(s, axis=-1, keepdims=True))

    p_self = jnp.exp2(s_self - m)
    l = p_self
    acc = p_self * vn2
    for s, vc_ref in zip(scores, vc_refs):
        p = jnp.exp2(s - m)
        l = l + jnp.sum(p, axis=-1, keepdims=True)
        acc = acc + _dot(p.astype(BF16), vc_ref[...].reshape(page * n_heads, dv).astype(BF16))

    o = acc / l
    o = o[0:n_heads, :] - _lam(lam_ref, lam_init) * o[n_heads:2 * n_heads, :]
    o_ref[...] = _subln(o, gs_ref[...], lam_init)


def _attn_kernel(pt_ref, lam_ref, gs_ref, q_ref, k_ref, v_ref, qd_ref, knd_ref, vnd_ref, ck_hbm, cv_hbm,
                 o_ref, od_ref, qq_s, ve_s, s_s, m_s, acc_s, kbuf, vbuf, sem, *, layer, tq, lam_init):
    n_pages = kbuf.shape[1]
    step = (pl.program_id(0) * pl.num_programs(1) + pl.program_id(1)) * pl.num_programs(2) + pl.program_id(2)
    n_steps = pl.num_programs(0) * pl.num_programs(1) * pl.num_programs(2)
    slot = lax.rem(step, 2)

    def page_copies(sample, slot):
        copies = []
        for p in range(n_pages):
            pg = pt_ref[sample, p]
            copies.append(pltpu.make_async_copy(ck_hbm.at[layer, pg], kbuf.at[slot, p], sem.at[slot, 0]))
            copies.append(pltpu.make_async_copy(cv_hbm.at[layer, pg], vbuf.at[slot, p], sem.at[slot, 1]))
        return copies

    @pl.when(step == 0)
    def _():
        for c in page_copies(0, 0):
            c.start()

    @pl.when(step + 1 < n_steps)
    def _():
        for c in page_copies(step + 1, 1 - slot):
            c.start()

    for c in page_copies(step, slot):
        c.wait()

    _decode_attention(lam_ref, gs_ref, qd_ref, knd_ref, vnd_ref,
                      [kbuf.at[slot, p] for p in range(n_pages)], [vbuf.at[slot, p] for p in range(n_pages)],
                      od_ref, lam_init=lam_init)
    _prompt_attention(lam_ref, gs_ref, q_ref, k_ref, v_ref, o_ref, qq_s, ve_s, s_s, m_s, acc_s,
                      tq=tq, lam_init=lam_init)


def _attention(qb, kb, vb, q3, kn3, vn3, cache_k, cache_v, page_table, lam4, gs, *, batch, seq, tq, layer, lam_init):
    m, dq = qb.shape
    n_heads = dq // V7X_LANES
    nq = seq // tq
    n_dec, _, dk = q3.shape
    n_pages = page_table.shape[1]
    page = cache_k.shape[2]
    dv = cache_v.shape[4]
    assert n_heads & (n_heads - 1) == 0 and dv == V7X_LANES
    assert n_dec == batch * n_heads * nq, "one decode sample per prompt-attention grid step"

    def sample(b, h, i):
        return (b * n_heads + h) * nq + i

    def sample_spec(d):
        return pl.BlockSpec((None, n_heads, d), lambda b, h, i, pt: (sample(b, h, i), 0, 0))

    tile_spec = pl.BlockSpec((tq, V7X_LANES), lambda b, h, i, pt: (b * nq + i, h))
    head_spec = pl.BlockSpec((seq, V7X_LANES), lambda b, h, i, pt: (b, h))
    hbm_spec = pl.BlockSpec(memory_space=pl.ANY)
    grid_spec = pltpu.PrefetchScalarGridSpec(
        num_scalar_prefetch=1,
        grid=(batch, n_heads, nq),
        in_specs=[
            pl.BlockSpec(lam4.shape, lambda b, h, i, pt: (0, 0)),
            pl.BlockSpec((1, dv), lambda b, h, i, pt: (0, 0)),
            tile_spec, head_spec, head_spec,
            sample_spec(dk), sample_spec(dk), sample_spec(dv),
            hbm_spec, hbm_spec,
        ],
        out_specs=[tile_spec, sample_spec(dv)],
        scratch_shapes=[pltpu.VMEM((2 * tq, V7X_LANES), BF16), pltpu.VMEM((seq, 2 * V7X_LANES), BF16),
                        pltpu.VMEM((2, 2 * tq, tq), F32), pltpu.VMEM((2 * tq, V7X_LANES), F32),
                        pltpu.VMEM((2 * tq, 2 * V7X_LANES), F32),
                        pltpu.VMEM((2, n_pages, page, n_heads, dk), F32),
                        pltpu.VMEM((2, n_pages, page, n_heads, dv), F32),
                        pltpu.SemaphoreType.DMA((2, 2))],
    )
    return pl.pallas_call(
        functools.partial(_attn_kernel, layer=layer, tq=tq, lam_init=lam_init),
        grid_spec=grid_spec,
        out_shape=[jax.ShapeDtypeStruct((m, dq), BF16), jax.ShapeDtypeStruct((n_dec, n_heads, dv), F32)],
        compiler_params=pltpu.CompilerParams(dimension_semantics=("arbitrary",) * 3,
                                             vmem_limit_bytes=ATTN_VMEM_LIMIT),
        name="attention",
    )(page_table, lam4, gs, qb, kb, vb, q3, kn3, vn3, cache_k, cache_v)


def _s5_discretize(are, aim, ldt):
    dt = jnp.exp(ldt)
    mag = jnp.exp(dt * are)
    lbr = mag * jnp.cos(dt * aim)
    lbi = mag * jnp.sin(dt * aim)
    nr = lbr - 1.0
    den = are * are + aim * aim
    cr = (nr * are + lbi * aim) / den
    ci = (lbi * are - nr * aim) / den
    return lbr, lbi, cr, ci


def _s5_input_weight(cr, ci, bre, bim):
    return jnp.concatenate([cr * bre - ci * bim, cr * bim + ci * bre], axis=1).astype(BF16)


def _s5_readout(hr, hi, cre_ref, cim_ref, d_ref, u):
    return (_dot_nt(hr.astype(BF16), cre_ref[...].astype(BF16))
            - _dot_nt(hi.astype(BF16), cim_ref[...].astype(BF16)) + d_ref[...] * u)


def _s5_prompt_kernel(are_ref, aim_ref, ldt_ref, bre_ref, bim_ref, cre_ref, cim_ref, d_ref, u_ref,
                      y_ref, hre_ref, him_ref, up_s, bur_s, bui_s, yp_s, pad_s, wb_s, lam_s, car_s, *, seg):
    ch = pl.program_id(2)
    n_st = bur_s.shape[1]
    sub = V7X_SUBLANES

    @pl.when(ch == 0)
    def _():
        lbr, lbi, cr, ci = _s5_discretize(are_ref[...], aim_ref[...], ldt_ref[...])
        wb_s[...] = _s5_input_weight(cr, ci, bre_ref[...], bim_ref[...])
        lam_s[0:1, :] = lbr
        lam_s[1:2, :] = lbi
        pr, pi = lbr, lbi
        for _ in range(int(math.log2(seg))):
            pr, pi = pr * pr - pi * pi, 2.0 * pr * pi
        lam_s[2:3, :] = pr
        lam_s[3:4, :] = pi
        car_s[...] = jnp.zeros_like(car_s)

    pitch = seg + sub
    for r in range(sub):
        pad_s[r * pitch:r * pitch + seg, :] = u_ref[r * seg:(r + 1) * seg, :]

    def permute(t, c):
        up_s[pl.ds(pl.multiple_of(t * sub, sub), sub), :] = pad_s[pl.ds(t, sub, stride=pitch), :]
        return c

    lax.fori_loop(0, seg, permute, 0, unroll=8)

    bu = _dot(up_s[...].astype(BF16), wb_s[...])
    bur_s[...] = bu[:, :n_st]
    bui_s[...] = bu[:, n_st:]

    lbr = jnp.broadcast_to(lam_s[0:1, :], (sub, n_st))
    lbi = jnp.broadcast_to(lam_s[1:2, :], (sub, n_st))

    def advance(t, c):
        sr, si = c
        r0 = pl.multiple_of(t * sub, sub)
        return (lbr * sr - lbi * si + bur_s[pl.ds(r0, sub), :],
                lbr * si + lbi * sr + bui_s[pl.ds(r0, sub), :])

    zero = jnp.zeros((sub, n_st), F32)
    er, ei = lax.fori_loop(0, seg, advance, (zero, zero), unroll=4)

    psr, psi = lam_s[2:3, :], lam_s[3:4, :]
    hr, hi = car_s[0:1, :], car_s[1:2, :]
    rows_r, rows_i = [], []
    for r in range(sub):
        rows_r.append(hr)
        rows_i.append(hi)
        hr, hi = (er[r:r + 1, :] + psr * hr - psi * hi, ei[r:r + 1, :] + psr * hi + psi * hr)
    car_s[0:1, :] = hr
    car_s[1:2, :] = hi

    def advance_store(t, c):
        sr, si = advance(t, c)
        r0 = pl.multiple_of(t * sub, sub)
        bur_s[pl.ds(r0, sub), :] = sr
        bui_s[pl.ds(r0, sub), :] = si
        return sr, si

    lax.fori_loop(0, seg, advance_store,
                  (jnp.concatenate(rows_r, axis=0), jnp.concatenate(rows_i, axis=0)), unroll=4)

    yp_s[...] = _s5_readout(bur_s[...], bui_s[...], cre_ref, cim_ref, d_ref, up_s[...])

    def unpermute(t, c):
        pad_s[pl.ds(t, sub, stride=pitch), :] = yp_s[pl.ds(pl.multiple_of(t * sub, sub), sub), :]
        return c

    lax.fori_loop(0, seg, unpermute, 0, unroll=8)
    for r in range(sub):
        y_ref[r * seg:(r + 1) * seg, :] = pad_s[r * pitch:r * pitch + seg, :]

    @pl.when(ch == pl.num_programs(2) - 1)
    def _():
        hre_ref[...] = hr
        him_ref[...] = hi


def _s5_prompt(u, lay, *, batch, seq, rows):
    m, d_ssm = u.shape
    are, aim, ldt, bre, bim, cre, cim, dsk = lay
    n_gb, n_ch_lanes, n_st = bre.shape
    n_ch = seq // rows
    seg = rows // V7X_SUBLANES
    lane_spec = pl.BlockSpec((None, 1, n_st), lambda b, g, c: (g, 0, 0))
    mat_spec = pl.BlockSpec((None, n_ch_lanes, n_st), lambda b, g, c: (g, 0, 0))
    row_spec = pl.BlockSpec((rows, n_ch_lanes), lambda b, g, c: (b * n_ch + c, g))
    st_spec = pl.BlockSpec((None, None, 1, n_st), lambda b, g, c: (b, g, 0, 0))
    st_shape = jax.ShapeDtypeStruct((batch, n_gb, 1, n_st), F32)
    return pl.pallas_call(
        functools.partial(_s5_prompt_kernel, seg=seg),
        grid=(batch, n_gb, n_ch),
        in_specs=[lane_spec, lane_spec, lane_spec, mat_spec, mat_spec, mat_spec, mat_spec,
                  pl.BlockSpec((1, n_ch_lanes), lambda b, g, c: (0, g)), row_spec],
        out_specs=[row_spec, st_spec, st_spec],
        out_shape=[jax.ShapeDtypeStruct((m, d_ssm), F32), st_shape, st_shape],
        scratch_shapes=[pltpu.VMEM((rows, n_ch_lanes), F32), pltpu.VMEM((rows, n_st), F32),
                        pltpu.VMEM((rows, n_st), F32), pltpu.VMEM((rows, n_ch_lanes), F32),
                        pltpu.VMEM((rows + V7X_SUBLANES * V7X_SUBLANES, n_ch_lanes), F32),
                        pltpu.VMEM((n_ch_lanes, 2 * n_st), BF16), pltpu.VMEM((4, n_st), F32),
                        pltpu.VMEM((2, n_st), F32)],
        compiler_params=_params(("parallel", "parallel", "arbitrary")),
        name="s5_prompt",
    )(are, aim, ldt, bre, bim, cre, cim, dsk, u)


def _s5_step_kernel(are_ref, aim_ref, ldt_ref, bre_ref, bim_ref, cre_ref, cim_ref, d_ref, u_ref,
                    h0r_ref, h0i_ref, y_ref, hre_ref, him_ref):
    n_st = h0r_ref.shape[1]
    lbr, lbi, cr, ci = _s5_discretize(are_ref[...], aim_ref[...], ldt_ref[...])
    u = u_ref[...]
    bu = _dot(u.astype(BF16), _s5_input_weight(cr, ci, bre_ref[...], bim_ref[...]))
    h0r, h0i = h0r_ref[...], h0i_ref[...]
    hr = lbr * h0r - lbi * h0i + bu[:, :n_st]
    hi = lbr * h0i + lbi * h0r + bu[:, n_st:]
    hre_ref[...] = hr
    him_ref[...] = hi
    y_ref[...] = _s5_readout(hr, hi, cre_ref, cim_ref, d_ref, u)


def _s5_step(u, h0r, h0i, lay):
    m, d_ssm = u.shape
    are, aim, ldt, bre, bim, cre, cim, dsk = lay
    n_gb, n_ch_lanes, n_st = bre.shape
    lane_spec = pl.BlockSpec((None, 1, n_st), lambda g: (g, 0, 0))
    mat_spec = pl.BlockSpec((None, n_ch_lanes, n_st), lambda g: (g, 0, 0))
    row_spec = pl.BlockSpec((m, n_ch_lanes), lambda g: (0, g))
    st_spec = pl.BlockSpec((m, n_st), lambda g: (0, g))
    st_shape = jax.ShapeDtypeStruct(h0r.shape, F32)
    return pl.pallas_call(
        _s5_step_kernel,
        grid=(n_gb,),
        in_specs=[lane_spec, lane_spec, lane_spec, mat_spec, mat_spec, mat_spec, mat_spec,
                  pl.BlockSpec((1, n_ch_lanes), lambda g: (0, g)), row_spec, st_spec, st_spec],
        out_specs=[row_spec, st_spec, st_spec],
        out_shape=[jax.ShapeDtypeStruct((m, d_ssm), F32), st_shape, st_shape],
        compiler_params=_params(("parallel",)),
        name="s5_step",
    )(are, aim, ldt, bre, bim, cre, cim, dsk, u, h0r, h0i)


def _s5_layout(a_re, a_im, log_dt, b_re, b_im, c_re, c_im, d_skip):
    n_g, n_st = a_re.shape
    gl = SSM_GROUP_BLOCK
    n_gb = n_g // gl
    same = jnp.eye(gl, dtype=jnp.bool_)[None, :, None, :, None]

    def lanes(a):
        return a.reshape(n_gb, 1, gl * n_st)

    def block_diag(w):
        w5 = w.reshape(n_gb, gl, SSM_GROUP, 1, n_st)
        return jnp.where(same, w5, 0.0).reshape(n_gb, gl * SSM_GROUP, gl * n_st)

    return (lanes(a_re), lanes(a_im), lanes(jnp.broadcast_to(log_dt[:, None], (n_g, n_st))),
            block_diag(b_re.transpose(0, 2, 1)), block_diag(b_im.transpose(0, 2, 1)),
            block_diag(c_re), block_diag(c_im), d_skip.reshape(1, n_g * SSM_GROUP))


def _mix_kernel(x_ref, ys_ref, o_ref, ga_ref, gb_ref, wglu_ref, wso_ref, wao_ref, wo_ref, out_ref):
    ys = ys_ref[...]
    ya = 0.5 * ys * (1.0 + lax.erf(ys * (2.0 ** -0.5)))
    glu = (ya * jax.nn.sigmoid(_dot(ya.astype(BF16), wglu_ref[...]))).astype(BF16)
    ssm = _dot(glu, wso_ref[...])
    att = _dot(o_ref[...], wao_ref[...])
    mix = (ga_ref[...].astype(F32) * ssm + gb_ref[...].astype(F32) * att).astype(BF16)
    out_ref[...] = x_ref[...] + _dot(mix, wo_ref[...])


def _mix(x, ys, o, gates, w_glu, w_ssm_out, w_attn_out, w_o, *, tm):
    m, d = x.shape
    d_ssm = ys.shape[1]
    d_att = o.shape[1]

    def resident(w):
        return pl.BlockSpec(w.shape, lambda i: (0, 0), pipeline_mode=pl.Buffered(1))

    return pl.pallas_call(
        _mix_kernel,
        grid=(m // tm,),
        in_specs=[
            pl.BlockSpec((tm, d), lambda i: (i, 0)),
            pl.BlockSpec((tm, d_ssm), lambda i: (i, 0)),
            pl.BlockSpec((tm, d_att), lambda i: (i, 0)),
            pl.BlockSpec((tm, d), lambda i: (i, 0)),
            pl.BlockSpec((tm, d), lambda i: (i, 1)),
            resident(w_glu), resident(w_ssm_out), resident(w_attn_out), resident(w_o),
        ],
        out_specs=pl.BlockSpec((tm, d), lambda i: (i, 0)),
        out_shape=jax.ShapeDtypeStruct((m, d), F32),
        compiler_params=_params(("parallel",)),
        name="mix",
    )(x, ys, o, gates, gates, w_glu, w_ssm_out, w_attn_out, w_o)


def _rotary_tables(pos, rows):
    half = ROT_DIM // 2
    inv = ROPE_THETA ** (-jnp.arange(half, dtype=F32) / half)
    ang = pos.astype(F32)[:, None] * inv[None, :]
    cos, sin = jnp.cos(ang), jnp.sin(ang)
    t = pos.shape[0]
    zh = jnp.zeros((t, half), F32)
    rest0 = jnp.zeros((t, HEAD_DIM - ROT_DIM), F32)
    cos_c = jnp.concatenate([cos, cos, jnp.ones((t, HEAD_DIM - ROT_DIM), F32)], axis=1)
    s1_c = jnp.concatenate([-sin, zh, rest0], axis=1)
    s2_c = jnp.concatenate([zh, sin, rest0], axis=1)
    return tuple(jnp.broadcast_to(jnp.tile(a, (1, 2)), (rows, 2 * HEAD_DIM)) for a in (cos_c, s1_c, s2_c))


def kernel(x_prompt, x_sample, cache_k, cache_v, state_ssm_re, state_ssm_im, page_table, norm_ffn1, ffn1_w1, ffn1_w3, ffn1_w2, norm_mix, w_in, lam_q1, lam_k1, lam_q2, lam_k2, g_subln, w_attn_out, ssm_a_re, ssm_a_im, ssm_log_dt, ssm_b_re, ssm_b_im, ssm_c_re, ssm_c_im, ssm_d, w_glu, w_ssm_out, w_o, norm_ffn2, ffn2_w1, ffn2_w3, ffn2_w2, norm_final):
    batch, seq, d = x_prompt.shape
    n_dec, dec_seq, _ = x_sample.shape
    assert dec_seq == 1
    depth = cache_k.shape[0]
    n_heads, d_k = cache_k.shape[3], cache_k.shape[4]
    d_v = cache_v.shape[4]
    n_groups, n_state = ssm_a_re.shape[1], ssm_a_re.shape[2]
    d_qk, d_att, d_ssm = n_heads * d_k, n_heads * d_v, n_groups * SSM_GROUP
    past_len = page_table.shape[1] * cache_k.shape[2]
    m_p = batch * seq

    tabs_p = _rotary_tables(jnp.arange(seq, dtype=jnp.int32), seq)
    tabs_s = _rotary_tables(past_len + jnp.arange(dec_seq, dtype=jnp.int32), n_dec)
    row = lambda a: a.reshape(1, -1)
    gfin = row(norm_final)

    h_p = x_prompt.reshape(m_p, d)
    h_s = x_sample.reshape(n_dec, d)
    outs = [[] for _ in range(8)]
    for l in range(depth):
        lam_init = 0.8 - 0.6 * math.exp(-0.3 * l)
        last = l == depth - 1
        wglu, wso = w_glu[l].astype(BF16), w_ssm_out[l].astype(BF16)
        wao, wo = w_attn_out[l].astype(BF16), w_o[l].astype(BF16)
        lam4 = jnp.stack([lam_q1[l], lam_k1[l], lam_q2[l], lam_k2[l]])
        gs = row(g_subln[l])
        lay = _s5_layout(ssm_a_re[l], ssm_a_im[l], ssm_log_dt[l], ssm_b_re[l], ssm_b_im[l],
                         ssm_c_re[l], ssm_c_im[l], ssm_d[l])

        def half1(x, tm, w):
            return _ffn(x, row(norm_ffn1[l]), *w, row(norm_mix[l]), norm_out="extra", tm=tm, tf=512)

        def half2(x, tm, w):
            return _ffn(x, row(norm_ffn2[l]), *w, gfin, norm_out="replace" if last else "none", tm=tm, tf=512)

        x1_s, xn_s, *w1_bf = half1(h_s, n_dec, (ffn1_w1[l], ffn1_w3[l], ffn1_w2[l]))
        q_s, k_s, _, v_s, _, u_s, gates_s, win_bf = _project(xn_s, w_in[l], *tabs_s, tm=n_dec,
                                                             d_qk=d_qk, d_v=d_att, d_ssm=d_ssm)
        x1_p, xn_p = half1(h_p, 512, w1_bf)
        q, k, kb, v, vb, u, gates = _project(xn_p, win_bf, *tabs_p, tm=1024,
                                             d_qk=d_qk, d_v=d_att, d_ssm=d_ssm)

        o, o_s = _attention(q, kb, vb, q_s.astype(F32).reshape(n_dec, n_heads, d_k),
                            k_s.reshape(n_dec, n_heads, d_k), v_s.reshape(n_dec, n_heads, d_v),
                            cache_k, cache_v, page_table, lam4, gs,
                            batch=batch, seq=seq, tq=512, layer=l, lam_init=lam_init)

        ys, hre, him = _s5_step(u_s, state_ssm_re[l].reshape(n_dec, n_groups * n_state),
                                state_ssm_im[l].reshape(n_dec, n_groups * n_state), lay)
        x2 = _mix(x1_s, ys, o_s.reshape(n_dec, d_att).astype(BF16), gates_s, wglu, wso, wao, wo, tm=n_dec)
        h_s, *w2_bf = half2(x2, n_dec, (ffn2_w1[l], ffn2_w3[l], ffn2_w2[l]))
        outs[4].append(k_s.reshape(n_dec, dec_seq, n_heads, d_k))
        outs[5].append(v_s.reshape(n_dec, dec_seq, n_heads, d_v))
        outs[6].append(hre.reshape(n_dec, n_groups, n_state))
        outs[7].append(him.reshape(n_dec, n_groups, n_state))

        ys, hre, him = _s5_prompt(u, lay, batch=batch, seq=seq, rows=seq)
        x2 = _mix(x1_p, ys, o, gates, wglu, wso, wao, wo, tm=256)
        (h_p,) = half2(x2, 512, w2_bf)
        outs[0].append(k.reshape(batch, seq, n_heads, d_k))
        outs[1].append(v.reshape(batch, seq, n_heads, d_v))
        outs[2].append(hre.reshape(batch, n_groups, n_state))
        outs[3].append(him.reshape(batch, n_groups, n_state))

    return (h_p.reshape(batch, seq, d), h_s.reshape(n_dec, dec_seq, d)) + tuple(jnp.stack(o) for o in outs)
```
